```python
import math
import jax
import jax.numpy as jnp
from jax import lax
import numpy as np

D_MODEL = 1024
BATCH = 8
SEQ = 8192
DEPTH = 2
DEC_BATCH = 16
DEC_SEQ = 64
PAST_LEN = 4096

CHUNK = 64
PLE_DIM = 256
N_EVEN = (DEPTH + 1) // 2
N_ODD = DEPTH // 2
EPS = 1e-6
F32 = jnp.float32
A_WIDTH = 512
A_GROUPS = 4
A_GROUP_DIM = A_WIDTH // A_GROUPS
MLP_CHUNK = 128
B_WIDTH = 512
B_GROUP_DIM = 16
B_GROUPS = B_WIDTH // B_GROUP_DIM
S5_STATE = 64
SCAN_BLOCK = 128
C_HEADS = 8
C_HEAD_DIM = 64
C_WIDTH = C_HEADS * C_HEAD_DIM
SB_BLOCK = 128
D_HEADS = 4
D_KEY_DIM = 128
D_VAL_DIM = 128
D_WIDTH = D_HEADS * D_VAL_DIM
CONV_W = 4
D_CONV_CH = 2 * D_HEADS * D_KEY_DIM + D_WIDTH
EVEN_IN = 3 * A_WIDTH + 2 * B_WIDTH
EVEN_MIX = A_WIDTH + B_WIDTH
ODD_IN = 4 * C_WIDTH + D_CONV_CH + D_WIDTH + 2 * D_HEADS
ODD_MIX = C_WIDTH + D_WIDTH

kernel_name = 'hybrid_streaming_encoder_step'


def rmsnorm(x, g):
    xf = x.astype(F32)
    var = jnp.mean(xf * xf, axis=-1, keepdims=True)
    return (xf * lax.rsqrt(var + EPS) * g.astype(F32)).astype(x.dtype)


def layernorm(x, g, b):
    xf = x.astype(F32)
    mu = jnp.mean(xf, axis=-1, keepdims=True)
    xc = xf - mu
    var = jnp.mean(xc * xc, axis=-1, keepdims=True)
    return (xc * lax.rsqrt(var + EPS) * g.astype(F32) + b.astype(F32)).astype(x.dtype)


def l2norm(x):
    xf = x.astype(F32)
    return xf * lax.rsqrt(jnp.sum(xf * xf, axis=-1, keepdims=True) + EPS)


def chunk_mlp_mix(u, v, w_s, b_s):
    bsz, L, _ = v.shape
    c = min(L, MLP_CHUNK)
    nc = L // c
    vg = v.reshape(bsz, nc, c, A_GROUPS, A_GROUP_DIM)
    mask = jnp.tril(jnp.ones((c, c), dtype=bool))
    w = jnp.where(mask, w_s[:, :c, :c], 0.0).astype(v.dtype)
    s = jnp.einsum('gts,bnsgd->bntgd', w, vg) + b_s[:, :c].T[None, None, :, :, None].astype(v.dtype)
    return u * s.reshape(bsz, L, A_WIDTH)


def s5_discretize(lam_re, lam_im, log_dt, b_re, b_im):
    dt = jnp.exp(log_dt.astype(F32))[:, None]
    lr, li = lam_re.astype(F32), lam_im.astype(F32)
    mag = jnp.exp(lr * dt)
    ar, ai = mag * jnp.cos(li * dt), mag * jnp.sin(li * dt)
    den = lr * lr + li * li
    cr = ((ar - 1.0) * lr + ai * li) / den
    ci = (ai * lr - (ar - 1.0) * li) / den
    br, bi = b_re.astype(F32), b_im.astype(F32)
    bbr = cr[..., None] * br - ci[..., None] * bi
    bbi = cr[..., None] * bi + ci[..., None] * br
    return ar, ai, bbr, bbi


def complex_affine_combine(e1, e2):
    ar1, ai1, br1, bi1 = e1
    ar2, ai2, br2, bi2 = e2
    return (ar2 * ar1 - ai2 * ai1, ar2 * ai1 + ai2 * ar1,
            ar2 * br1 - ai2 * bi1 + br2, ar2 * bi1 + ai2 * br1 + bi2)


def s5_mix(u, x0_re, x0_im, lam_re, lam_im, log_dt, b_re, b_im, c_re, c_im, d_skip):
    bsz, L, _ = u.shape
    ar, ai, bbr, bbi = s5_discretize(lam_re, lam_im, log_dt, b_re, b_im)
    blk = min(L, SCAN_BLOCK)
    nb = L // blk
    ug = u.astype(F32).reshape(bsz, nb, blk, B_GROUPS, B_GROUP_DIM).swapaxes(0, 1)
    a_re = jnp.broadcast_to(ar, (bsz, blk, B_GROUPS, S5_STATE))
    a_im = jnp.broadcast_to(ai, (bsz, blk, B_GROUPS, S5_STATE))
    cr, ci, dk = c_re.astype(F32), c_im.astype(F32), d_skip.astype(F32)

    def block(carry, ub):
        xr0, xi0 = carry
        bur = jnp.einsum('btgp,gnp->btgn', ub, bbr)
        bui = jnp.einsum('btgp,gnp->btgn', ub, bbi)
        pr, pi, lr_, li_ = lax.associative_scan(complex_affine_combine, (a_re, a_im, bur, bui), axis=1)
        xr = pr * xr0[:, None] - pi * xi0[:, None] + lr_
        xi = pr * xi0[:, None] + pi * xr0[:, None] + li_
        y = jnp.einsum('btgn,gpn->btgp', xr, cr) - jnp.einsum('btgn,gpn->btgp', xi, ci) + dk * ub
        return (xr[:, -1], xi[:, -1]), y

    (xr, xi), y = lax.scan(block, (x0_re.astype(F32), x0_im.astype(F32)), ug)
    y = y.swapaxes(0, 1).reshape(bsz, L, B_WIDTH)
    return y.astype(u.dtype), xr, xi


def stick_breaking_attention(q, k, v, q_offset):
    bsz, L, H, d = q.shape
    S = k.shape[1]
    blk = min(L, SB_BLOCK)
    nb = L // blk
    qb = q.reshape(bsz, nb, blk, H, d).transpose(1, 0, 3, 2, 4)
    kt = k.transpose(0, 2, 1, 3)
    vt = v.transpose(0, 2, 1, 3)
    kpos = jnp.arange(S)
    scale = 1.0 / math.sqrt(d)

    def one_block(args):
        j, qj = args
        qpos = q_offset + j * blk + jnp.arange(blk)
        z = jnp.einsum('bhtd,bhsd->bhts', qj, kt).astype(F32) * scale
        mask = kpos[None, :] < qpos[:, None]
        log_beta = jax.nn.log_sigmoid(z)
        log_keep = jnp.where(mask, log_beta - z, 0.0)
        after = lax.cumsum(log_keep, axis=3, reverse=True) - log_keep
        w = jnp.where(mask, jnp.exp(log_beta + after), 0.0)
        return jnp.einsum('bhts,bhsd->bhtd', w.astype(vt.dtype), vt)

    o = lax.map(one_block, (jnp.arange(nb), qb))
    return o.transpose(1, 0, 3, 2, 4).reshape(bsz, L, H * d)


def causal_conv(x, buf, w):
    L = x.shape[1]
    xp = jnp.concatenate([buf.astype(x.dtype), x], axis=1)
    y = xp[:, 0:L] * w[0]
    for i in range(1, CONV_W):
        y = y + xp[:, i:i + L] * w[i]
    return y, xp[:, -(CONV_W - 1):]


def gdn_chunk(S, q, k, v, g, beta):
    c = q.shape[-2]
    causal = jnp.tril(jnp.ones((c, c), dtype=bool))
    strict = jnp.tril(jnp.ones((c, c), dtype=bool), k=-1)
    G = jnp.cumsum(g, axis=-1)
    decay = jnp.exp(jnp.where(causal, G[..., :, None] - G[..., None, :], -jnp.inf))
    kk = jnp.einsum('bhtd,bhsd->bhts', k, k)
    tri = jnp.where(strict, beta[..., :, None] * decay * kk, 0.0) + jnp.eye(c, dtype=F32)
    rhs = beta[..., None] * (v - jnp.exp(G)[..., None] * jnp.einsum('bhtd,bhde->bhte', k, S))
    delta = lax.linalg.triangular_solve(tri, rhs, left_side=True, lower=True, unit_diagonal=True)
    qk = jnp.einsum('bhtd,bhsd->bhts', q, k) * decay
    o = jnp.exp(G)[..., None] * jnp.einsum('bhtd,bhde->bhte', q, S) + jnp.einsum('bhts,bhse->bhte', qk, delta)
    g_last = G[..., -1:]
    S_new = jnp.exp(g_last)[..., None] * S + jnp.einsum('bhsd,bhse->bhde', k * jnp.exp(g_last - G)[..., None], delta)
    return S_new, o


def gated_delta_mix(q, k, v, g, beta, S0):
    bsz, L = q.shape[:2]
    c = min(L, CHUNK)
    nc = L // c

    def split(x):
        x = x.reshape((bsz, nc, c) + x.shape[2:])
        return jnp.moveaxis(jnp.moveaxis(x, 1, 0), 2, 3)

    S, o = lax.scan(lambda s, xs: gdn_chunk(s, *xs), S0, (split(q), split(k), split(v), split(g), split(beta)))
    o = jnp.moveaxis(jnp.moveaxis(o, 3, 2), 0, 1).reshape(bsz, L, D_HEADS, D_VAL_DIM)
    return o, S


def run_trunk(x, p, past_b_re, past_b_im, past_k_c, past_v_c, past_s_d, past_conv_d,
              norm_g, final_norm_g, ple_proj, ple_gate_w, ple_norm_g,
              even_w_in, even_w_out, a_ln_g, a_ln_b, a_w_s, a_b_s,
              b_lam_re, b_lam_im, b_log_dt, b_B_re, b_B_im, b_C_re, b_C_im, b_D, b_glu_w,
              odd_w_in, odd_w_out, d_conv_w, d_A_log, d_dt_bias, d_norm_g):
    bsz, L, _ = x.shape
    h = x
    nb_re, nb_im, na_v, nk, nv, ns, nconv = [], [], [], [], [], [], []
    for i in range(DEPTH):
        j = i // 2
        hn = rmsnorm(h, norm_g[i])
        if i % 2 == 0:
            proj = hn @ even_w_in[j]
            ua, va, za, ub, zb = jnp.split(
                proj, [A_WIDTH, 2 * A_WIDTH, 3 * A_WIDTH, 3 * A_WIDTH + B_WIDTH], axis=-1)
            ua = jax.nn.gelu(ua)
            va = layernorm(jax.nn.gelu(va), a_ln_g[j], a_ln_b[j])
            a_out = chunk_mlp_mix(ua, va, a_w_s[j], a_b_s[j]) * jax.nn.silu(za)
            if past_b_re is None:
                x0r = jnp.zeros((bsz, B_GROUPS, S5_STATE), F32)
                x0i = jnp.zeros((bsz, B_GROUPS, S5_STATE), F32)
            else:
                x0r, x0i = past_b_re[j], past_b_im[j]
            yb, xr, xi = s5_mix(ub, x0r, x0i, b_lam_re[j], b_lam_im[j], b_log_dt[j],
                                b_B_re[j], b_B_im[j], b_C_re[j], b_C_im[j], b_D[j])
            yb = jax.nn.gelu(yb)
            yb = yb * jax.nn.sigmoid(yb @ b_glu_w[j])
            b_out = yb * jax.nn.silu(zb)
            mix = jnp.concatenate([a_out, b_out], axis=-1) @ even_w_out[j]
            na_v.append(va)
            nb_re.append(xr)
            nb_im.append(xi)
        else:
            o1 = 4 * C_WIDTH
            o2 = o1 + D_CONV_CH
            o3 = o2 + D_WIDTH
            o4 = o3 + D_HEADS
            proj = hn @ odd_w_in[j]
            qc, kc, vc, zc, qkv_d, zd, a_d, b_d = jnp.split(
                proj, [C_WIDTH, 2 * C_WIDTH, 3 * C_WIDTH, o1, o2, o3, o4], axis=-1)
            qc = qc.reshape(bsz, L, C_HEADS, C_HEAD_DIM)
            kc = kc.reshape(bsz, L, C_HEADS, C_HEAD_DIM)
            vc = vc.reshape(bsz, L, C_HEADS, C_HEAD_DIM)
            if past_k_c is None:
                k_all, v_all, offset = kc, vc, 0
            else:
                pk, pv = past_k_c[j], past_v_c[j]
                k_all = jnp.concatenate([pk.astype(kc.dtype), kc], axis=1)
                v_all = jnp.concatenate([pv.astype(vc.dtype), vc], axis=1)
                offset = pk.shape[1]
            c_out = stick_breaking_attention(qc, k_all, v_all, offset) * jax.nn.silu(zc)
            if past_conv_d is None:
                buf = jnp.zeros((bsz, CONV_W - 1, D_CONV_CH), x.dtype)
                S0 = jnp.zeros((bsz, D_HEADS, D_KEY_DIM, D_VAL_DIM), F32)
            else:
                buf = past_conv_d[j]
                S0 = past_s_d[j].astype(F32)
            qkv_d, conv_new = causal_conv(qkv_d, buf, d_conv_w[j])
            qkv_d = jax.nn.silu(qkv_d)
            qd, kd, vd = jnp.split(qkv_d, [D_HEADS * D_KEY_DIM, 2 * D_HEADS * D_KEY_DIM], axis=-1)
            qd = l2norm(qd.reshape(bsz, L, D_HEADS, D_KEY_DIM)) * (D_KEY_DIM ** -0.5)
            kd = l2norm(kd.reshape(bsz, L, D_HEADS, D_KEY_DIM))
            vd = vd.reshape(bsz, L, D_HEADS, D_VAL_DIM).astype(F32)
            g = -jnp.exp(d_A_log[j].astype(F32)) * jax.nn.softplus(a_d.astype(F32) + d_dt_bias[j].astype(F32))
            beta = jax.nn.sigmoid(b_d.astype(F32))
            o_d, S = gated_delta_mix(qd, kd, vd, g, beta, S0)
            o_d = rmsnorm(o_d, d_norm_g[j]) * jax.nn.silu(zd.reshape(bsz, L, D_HEADS, D_VAL_DIM).astype(F32))
            d_out = o_d.reshape(bsz, L, D_WIDTH).astype(x.dtype)
            mix = jnp.concatenate([c_out, d_out], axis=-1) @ odd_w_out[j]
            nk.append(kc)
            nv.append(vc)
            ns.append(S)
            nconv.append(conv_new)
        h = h + mix
        gate = jax.nn.sigmoid(rmsnorm(h, ple_norm_g[i]) @ ple_gate_w[i])
        h = h + gate * (p[i] @ ple_proj[i])
    y = rmsnorm(h, final_norm_g)
    return (y, jnp.stack(nb_re), jnp.stack(nb_im), jnp.stack(na_v), jnp.stack(nk), jnp.stack(nv),
            jnp.stack(ns), jnp.stack(nconv))


def setup_inputs(seed: int = 0) -> dict:
    key = jax.random.key(seed)
    ks = iter(jax.random.split(key, 48))

    def nrm(shape, scale):
        return jax.random.normal(next(ks), shape, F32) * scale

    def unif(shape, lo, hi):
        return jax.random.uniform(next(ks), shape, F32, lo, hi)

    dt_d = jnp.exp(unif((N_ODD, D_HEADS), math.log(1e-3), math.log(1e-1)))
    return {
        'x_prompt': nrm((BATCH, SEQ, D_MODEL), 1.0),
        'x_sample': nrm((DEC_BATCH, DEC_SEQ, D_MODEL), 1.0),
        'state_b_re': nrm((N_EVEN, DEC_BATCH, B_GROUPS, S5_STATE), 0.5),
        'state_b_im': nrm((N_EVEN, DEC_BATCH, B_GROUPS, S5_STATE), 0.5),
        'cache_k_c': nrm((N_ODD, DEC_BATCH, PAST_LEN, C_HEADS, C_HEAD_DIM), 1.0),
        'cache_v_c': nrm((N_ODD, DEC_BATCH, PAST_LEN, C_HEADS, C_HEAD_DIM), 1.0),
        'state_d': nrm((N_ODD, DEC_BATCH, D_HEADS, D_KEY_DIM, D_VAL_DIM), 0.3),
        'state_conv_d': nrm((N_ODD, DEC_BATCH, CONV_W - 1, D_CONV_CH), 1.0),
        'p_prompt': nrm((DEPTH, BATCH, SEQ, PLE_DIM), 1.0),
        'p_sample': nrm((DEPTH, DEC_BATCH, DEC_SEQ, PLE_DIM), 1.0),
        'norm_g': 1.0 + nrm((DEPTH, D_MODEL), 0.02),
        'final_norm_g': 1.0 + nrm((D_MODEL,), 0.02),
        'ple_proj': nrm((DEPTH, PLE_DIM, D_MODEL), 0.5 * PLE_DIM ** -0.5),
        'ple_gate_w': nrm((DEPTH, D_MODEL, D_MODEL), D_MODEL ** -0.5),
        'ple_norm_g': 1.0 + nrm((DEPTH, D_MODEL), 0.02),
        'even_w_in': nrm((N_EVEN, D_MODEL, EVEN_IN), D_MODEL ** -0.5),
        'even_w_out': nrm((N_EVEN, EVEN_MIX, D_MODEL), 0.5 * EVEN_MIX ** -0.5),
        'a_ln_g': 1.0 + nrm((N_EVEN, A_WIDTH), 0.02),
        'a_ln_b': nrm((N_EVEN, A_WIDTH), 0.02),
        'a_w_s': nrm((N_EVEN, A_GROUPS, MLP_CHUNK, MLP_CHUNK), MLP_CHUNK ** -0.5),
        'a_b_s': 1.0 + nrm((N_EVEN, A_GROUPS, MLP_CHUNK), 0.1),
        'b_lam_re': -0.5 + nrm((N_EVEN, B_GROUPS, S5_STATE), 0.01),
        'b_lam_im': math.pi * jnp.arange(S5_STATE, dtype=F32) + nrm((N_EVEN, B_GROUPS, S5_STATE), 0.01),
        'b_log_dt': unif((N_EVEN, B_GROUPS), math.log(1e-3), math.log(1e-1)),
        'b_B_re': nrm((N_EVEN, B_GROUPS, S5_STATE, B_GROUP_DIM), (2 * B_GROUP_DIM) ** -0.5),
        'b_B_im': nrm((N_EVEN, B_GROUPS, S5_STATE, B_GROUP_DIM), (2 * B_GROUP_DIM) ** -0.5),
        'b_C_re': nrm((N_EVEN, B_GROUPS, B_GROUP_DIM, S5_STATE), S5_STATE ** -0.5),
        'b_C_im': nrm((N_EVEN, B_GROUPS, B_GROUP_DIM, S5_STATE), S5_STATE ** -0.5),
        'b_D': nrm((N_EVEN, B_GROUPS, B_GROUP_DIM), 0.5),
        'b_glu_w': nrm((N_EVEN, B_WIDTH, B_WIDTH), B_WIDTH ** -0.5),
        'odd_w_in': nrm((N_ODD, D_MODEL, ODD_IN), D_MODEL ** -0.5),
        'odd_w_out': nrm((N_ODD, ODD_MIX, D_MODEL), 0.5 * ODD_MIX ** -0.5),
        'd_conv_w': nrm((N_ODD, CONV_W, D_CONV_CH), CONV_W ** -0.5),
        'd_A_log': jnp.log(unif((N_ODD, D_HEADS), 1.0, 16.0)),
        'd_dt_bias': dt_d + jnp.log(-jnp.expm1(-dt_d)),
        'd_norm_g': 1.0 + nrm((N_ODD, D_VAL_DIM), 0.02),
    }


def reference(x_prompt, x_sample, state_b_re, state_b_im, cache_k_c, cache_v_c, state_d, state_conv_d,
              p_prompt, p_sample,
              norm_g, final_norm_g, ple_proj, ple_gate_w, ple_norm_g,
              even_w_in, even_w_out, a_ln_g, a_ln_b, a_w_s, a_b_s,
              b_lam_re, b_lam_im, b_log_dt, b_B_re, b_B_im, b_C_re, b_C_im, b_D, b_glu_w,
              odd_w_in, odd_w_out, d_conv_w, d_A_log, d_dt_bias, d_norm_g):
    y_prompt, b_re_p, b_im_p, _a_v_p, k_c_p, v_c_p, s_d_p, conv_d_p = run_trunk(
        x_prompt, p_prompt, None, None, None, None, None, None,
        norm_g, final_norm_g, ple_proj, ple_gate_w, ple_norm_g,
        even_w_in, even_w_out, a_ln_g, a_ln_b, a_w_s, a_b_s,
        b_lam_re, b_lam_im, b_log_dt, b_B_re, b_B_im, b_C_re, b_C_im, b_D, b_glu_w,
        odd_w_in, odd_w_out, d_conv_w, d_A_log, d_dt_bias, d_norm_g)
    y_sample, b_re_s, b_im_s, a_v_s, k_c_s, v_c_s, s_d_s, conv_d_s = run_trunk(
        x_sample, p_sample, state_b_re, state_b_im, cache_k_c, cache_v_c, state_d, state_conv_d,
        norm_g, final_norm_g, ple_proj, ple_gate_w, ple_norm_g,
        even_w_in, even_w_out, a_ln_g, a_ln_b, a_w_s, a_b_s,
        b_lam_re, b_lam_im, b_log_dt, b_B_re, b_B_im, b_C_re, b_C_im, b_D, b_glu_w,
        odd_w_in, odd_w_out, d_conv_w, d_A_log, d_dt_bias, d_norm_g)
    return (y_prompt, y_sample,
            b_re_p, b_im_p, k_c_p, v_c_p, s_d_p, conv_d_p,
            b_re_s, b_im_s, a_v_s, k_c_s, v_c_s, s_d_s, conv_d_s)
```

```python
import functools
import math

import jax
import jax.numpy as jnp
from jax import lax
from jax.experimental import pallas as pl
from jax.experimental.pallas import tpu as pltpu

F32 = jnp.float32
BF16 = jnp.bfloat16
EPS = 1e-6

D_MODEL = 1024
PLE_DIM = 256
MIX_W = 512
A_GROUPS = 4
MLP_CHUNK = 128
S5_GROUPS = 32
S5_P = 16
S5_N = 64
S5_LANES = S5_GROUPS * S5_N
SB_HEADS = 8
SB_DIM = 64
SB_BLOCK = 128
GDN_HEADS = 4
GDN_DIM = 128
GDN_CHUNK = 64
CONV_W = 4
CONV_CH = 3 * MIX_W
LANE = 128
SUBLANE = 8
VMEM_LIMIT = 52 * 1024 * 1024
SB_DEAD = -104.0
HI = lax.Precision.HIGHEST


def _cparams(sem):
    return pltpu.CompilerParams(dimension_semantics=sem, vmem_limit_bytes=VMEM_LIMIT)


def _gelu(x):
    return 0.5 * x * (1.0 + jnp.tanh(0.7978845608028654 * (x + 0.044715 * (x * x * x))))


def _sigmoid(x):
    return 1.0 / (1.0 + jnp.exp(-x))


def _silu(x):
    return x * _sigmoid(x)


def _softplus(x):
    return jnp.maximum(x, 0.0) + jnp.log1p(jnp.exp(-jnp.abs(x)))


def _rms(x, g):
    ms = jnp.mean(x * x, axis=-1, keepdims=True)
    return x * lax.rsqrt(ms + EPS) * g


def _dot(a, b):
    return jnp.dot(a, b, preferred_element_type=F32)


def _dot_nt(a, b):
    return lax.dot_general(a, b, (((1,), (1,)), ((), ())), preferred_element_type=F32)


def _dot_tn(a, b):
    return lax.dot_general(a, b, (((0,), (0,)), ((), ())), preferred_element_type=F32)


def _split3(x):
    hi = x.astype(BF16)
    r = x - hi.astype(F32)
    mid = r.astype(BF16)
    lo = (r - mid.astype(F32)).astype(BF16)
    return hi, mid, lo


def _full(shape):
    n = len(shape)
    return pl.BlockSpec(shape, lambda *_: (0,) * n)


def _even_in_body(x_ref, g_ref, w_ref, lng_ref, lnb_ref, ws_ref, bs_ref,
                  aout_ref, ub_ref, szb_ref, av_ref, *, chunk):
    tm = x_ref.shape[0]
    hn = _rms(x_ref[...], g_ref[...]).astype(BF16)

    def proj(k):
        return _dot(hn, w_ref[:, k * MIX_W:(k + 1) * MIX_W])

    va = _gelu(proj(1))
    mu = jnp.mean(va, axis=-1, keepdims=True)
    vc = va - mu
    var = jnp.mean(vc * vc, axis=-1, keepdims=True)
    va = vc * lax.rsqrt(var + EPS) * lng_ref[...] + lnb_ref[...]
    if av_ref is not None:
        av_ref[...] = va
    va_bf = va.astype(BF16)
    gate = _gelu(proj(0)) * _silu(proj(2))
    gw = MIX_W // A_GROUPS
    for ch in range(tm // chunk):
        rows = slice(ch * chunk, (ch + 1) * chunk)
        for g in range(A_GROUPS):
            cols = slice(g * gw, (g + 1) * gw)
            s = _dot(ws_ref[g], va_bf[rows, cols]) + bs_ref[g]
            aout_ref[rows, cols] = (gate[rows, cols] * s).astype(aout_ref.dtype)
    ub_ref[...] = proj(3)
    szb_ref[...] = _silu(proj(4))


def _even_in(x2d, norm_g, w_in, ln_g, ln_b, w_s, b_s, *, nb, chunk, tm, want_av):
    n = x2d.shape[0]
    nt = n // (nb * tm)
    body = functools.partial(_even_in_body, chunk=chunk)
    if not want_av:
        body_fn = lambda *r: body(*r, None)
    else:
        body_fn = body
    row = lambda b, i: (b * nt + i, 0)
    tmaj = lambda b, i: (i, b)
    out_shape = [jax.ShapeDtypeStruct((n, MIX_W), BF16),
                 jax.ShapeDtypeStruct((nt * tm, nb * MIX_W), F32),
                 jax.ShapeDtypeStruct((nt * tm, nb * MIX_W), F32)]
    out_specs = [pl.BlockSpec((tm, MIX_W), row),
                 pl.BlockSpec((tm, MIX_W), tmaj),
                 pl.BlockSpec((tm, MIX_W), tmaj)]
    if want_av:
        out_shape.append(jax.ShapeDtypeStruct((n, MIX_W), F32))
        out_specs.append(pl.BlockSpec((tm, MIX_W), row))
    return pl.pallas_call(
        body_fn,
        grid=(nb, nt),
        in_specs=[pl.BlockSpec((tm, D_MODEL), row),
                  _full((1, D_MODEL)),
                  _full(w_in.shape),
                  _full((1, MIX_W)), _full((1, MIX_W)),
                  _full(w_s.shape), _full(b_s.shape)],
        out_specs=out_specs,
        out_shape=out_shape,
        compiler_params=_cparams(("parallel", "parallel")),
        name="even_in",
    )(x2d, norm_g, w_in, ln_g, ln_b, w_s, b_s)


def _s5_disc_body(lr_ref, li_ref, ldt_ref, br_ref, bi_ref, ar_ref, ai_ref, bbr_ref, bbi_ref):
    dt = jnp.exp(ldt_ref[...])
    lr, li = lr_ref[...], li_ref[...]
    mag = jnp.exp(lr * dt)
    ar = mag * jnp.cos(li * dt)
    ai = mag * jnp.sin(li * dt)
    den = lr * lr + li * li
    cr = ((ar - 1.0) * lr + ai * li) / den
    ci = (ai * lr - (ar - 1.0) * li) / den
    ar_ref[...] = ar
    ai_ref[...] = ai
    for p in range(S5_P):
        br, bi = br_ref[p], bi_ref[p]
        bbr_ref[p] = cr * br - ci * bi
        bbi_ref[p] = cr * bi + ci * br


def _s5_disc(lam_re, lam_im, log_dt, b_re, b_im):
    gn = (S5_GROUPS, S5_N)
    pgn = (S5_P, S5_GROUPS, S5_N)
    return pl.pallas_call(
        _s5_disc_body,
        out_shape=[jax.ShapeDtypeStruct(gn, F32), jax.ShapeDtypeStruct(gn, F32),
                   jax.ShapeDtypeStruct(pgn, F32), jax.ShapeDtypeStruct(pgn, F32)],
        name="s5_disc",
    )(lam_re, lam_im, log_dt.reshape(S5_GROUPS, 1),
      jnp.transpose(b_re, (2, 0, 1)), jnp.transpose(b_im, (2, 0, 1)))


def _s5_block_diag(bb_pgn, c_re, c_im):
    gpc = LANE // S5_P
    nj = S5_GROUPS // gpc
    eye = jnp.eye(gpc, dtype=F32)

    def b_blk(bb):
        b = jnp.transpose(bb, (1, 0, 2)).reshape(nj, gpc, S5_P, S5_N)
        m = b[:, :, :, None, :] * eye[None, :, None, :, None]
        return m.reshape(nj, gpc * S5_P, gpc * S5_N).astype(BF16)

    def c_blk(c):
        cc = jnp.transpose(c, (0, 2, 1)).reshape(nj, gpc, S5_N, S5_P)
        m = cc[:, :, :, None, :] * eye[None, :, None, :, None]
        return m.reshape(nj, gpc * S5_N, gpc * S5_P).astype(BF16)

    return b_blk(bb_pgn[0]), b_blk(bb_pgn[1]), c_blk(c_re), c_blk(-c_im)


def _s5_body(u_ref, szb_ref, x0r_ref, x0i_ref, ar_ref, ai_ref, bre_ref, bim_ref, cre_ref, cim_ref,
             d_ref, glu_ref, out_ref, st_r, st_i, sre, sim):
    t_steps = u_ref.shape[0]
    rows = t_steps * SUBLANE
    ti = pl.program_id(1)
    cw = bre_ref.shape[2]
    nj = bre_ref.shape[0]

    @pl.when(ti == 0)
    def _():
        st_r[...] = x0r_ref[...]
        st_i[...] = x0i_ref[...]

    u = u_ref[...].reshape(rows, MIX_W)
    u_bf = u.astype(BF16)
    for j in range(nj):
        uj = u_bf[:, j * LANE:(j + 1) * LANE]
        sre[:, j * cw:(j + 1) * cw] = _dot(uj, bre_ref[j])
        sim[:, j * cw:(j + 1) * cw] = _dot(uj, bim_ref[j])

    for j in range(nj):
        lanes = slice(j * cw, (j + 1) * cw)
        a_r = jnp.broadcast_to(ar_ref[:, lanes], (SUBLANE, cw))
        a_i = jnp.broadcast_to(ai_ref[:, lanes], (SUBLANE, cw))

        def step(t, carry, lanes=lanes, a_r=a_r, a_i=a_i):
            xr, xi = carry
            r0 = pl.multiple_of(t * SUBLANE, SUBLANE)
            nxr = a_r * xr - a_i * xi + sre[pl.ds(r0, SUBLANE), lanes]
            nxi = a_r * xi + a_i * xr + sim[pl.ds(r0, SUBLANE), lanes]
            sre[pl.ds(r0, SUBLANE), lanes] = nxr
            sim[pl.ds(r0, SUBLANE), lanes] = nxi
            return nxr, nxi

        xr, xi = lax.fori_loop(0, t_steps, step, (st_r[:, lanes], st_i[:, lanes]), unroll=8)
        st_r[:, lanes] = xr
        st_i[:, lanes] = xi

    ys = []
    for j in range(nj):
        lanes = slice(j * cw, (j + 1) * cw)
        ys.append(_dot(sre[:, lanes].astype(BF16), cre_ref[j])
                  + _dot(sim[:, lanes].astype(BF16), cim_ref[j]))
    y = jnp.concatenate(ys, axis=1) + d_ref[...] * u
    y = _gelu(y)
    y = y * _sigmoid(_dot(y.astype(BF16), glu_ref[...]))
    y = y * szb_ref[...].reshape(rows, MIX_W)
    out_ref[...] = y.reshape(out_ref.shape)


def _s5(u_tm, szb_tm, x0r, x0i, ar, ai, bre, bim, cre, cim, d_skip, glu_w, *, t_steps):
    length, nb, _ = u_tm.shape
    blk = pl.BlockSpec((t_steps, SUBLANE, MIX_W), lambda b, t: (t, b, 0))
    st = pl.BlockSpec((SUBLANE, S5_LANES), lambda b, t: (b, 0))
    rows = t_steps * SUBLANE
    return pl.pallas_call(
        _s5_body,
        grid=(nb // SUBLANE, length // t_steps),
        in_specs=[blk, blk, st, st, _full((1, S5_LANES)), _full((1, S5_LANES)),
                  _full(bre.shape), _full(bim.shape), _full(cre.shape), _full(cim.shape),
                  _full((1, MIX_W)), _full(glu_w.shape)],
        out_specs=[blk, st, st],
        out_shape=[jax.ShapeDtypeStruct((length, nb, MIX_W), F32),
                   jax.ShapeDtypeStruct((nb, S5_LANES), F32),
                   jax.ShapeDtypeStruct((nb, S5_LANES), F32)],
        scratch_shapes=[pltpu.VMEM((rows, S5_LANES), F32), pltpu.VMEM((rows, S5_LANES), F32)],
        compiler_params=_cparams(("parallel", "arbitrary")),
        name="s5_scan",
    )(u_tm, szb_tm, x0r, x0i, ar, ai, bre, bim, cre, cim, d_skip, glu_w)


def _mix_out_body(a_ref, b_ref, h_ref, p_ref, wout_ref, pg_ref, wgate_ref, wp_ref, fg_ref, out_ref):
    mix = (_dot(a_ref[...].astype(BF16), wout_ref[:MIX_W, :])
           + _dot(b_ref[...].astype(BF16), wout_ref[MIX_W:, :]))
    h1 = h_ref[...] + mix
    gate = _sigmoid(_dot(_rms(h1, pg_ref[...]).astype(BF16), wgate_ref[...]))
    h2 = h1 + gate * _dot(p_ref[...].astype(BF16), wp_ref[...])
    if fg_ref is not None:
        h2 = _rms(h2, fg_ref[...])
    out_ref[...] = h2


def _mix_out(a2d, b2d, h2d, p2d, w_out, ple_g, w_gate, w_p, final_g, *, nb, tm, b_time_major):
    n = h2d.shape[0]
    nt = n // (nb * tm)
    row = lambda b, i: (b * nt + i, 0)
    tmaj = lambda b, i: (i, b)
    args = [a2d, b2d, h2d, p2d, w_out, ple_g, w_gate, w_p]
    in_specs = [pl.BlockSpec((tm, MIX_W), row),
                pl.BlockSpec((tm, MIX_W), tmaj if b_time_major else row),
                pl.BlockSpec((tm, D_MODEL), row),
                pl.BlockSpec((tm, PLE_DIM), row),
                _full(w_out.shape), _full((1, D_MODEL)), _full(w_gate.shape), _full(w_p.shape)]
    if final_g is None:
        body = lambda *r: _mix_out_body(*r[:8], None, r[8])
    else:
        body = _mix_out_body
        args.append(final_g)
        in_specs.append(_full((1, D_MODEL)))
    return pl.pallas_call(
        body,
        grid=(nb, nt),
        in_specs=in_specs,
        out_specs=pl.BlockSpec((tm, D_MODEL), row),
        out_shape=jax.ShapeDtypeStruct((n, D_MODEL), F32),
        compiler_params=_cparams(("parallel", "parallel")),
        name="mix_out",
    )(*args)


def _odd_in_body(h_ref, g_ref, w_ref, wab_ref, alog_ref, dtb_ref,
                 q_ref, k_ref, v_ref, szc_ref, qkv_ref, szd_ref, gb_ref):
    hn = _rms(h_ref[...], g_ref[...]).astype(BF16)

    def proj(k, width=MIX_W):
        return _dot(hn, w_ref[:, k * MIX_W:k * MIX_W + width])

    q_ref[...] = (proj(0) * (SB_DIM ** -0.5)).astype(BF16)
    k_ref[...] = proj(1)
    v_ref[...] = proj(2)
    szc_ref[...] = _silu(proj(3))
    for k in range(3):
        qkv_ref[:, k * MIX_W:(k + 1) * MIX_W] = proj(4 + k)
    szd_ref[...] = _silu(proj(7))
    ab = _dot(hn, wab_ref[...])
    g = -jnp.exp(alog_ref[...]) * _softplus(ab + dtb_ref[...])
    lane = lax.broadcasted_iota(jnp.int32, ab.shape, 1)
    gb_ref[...] = jnp.where(lane < GDN_HEADS, g, _sigmoid(ab))


def _odd_in(h2d, norm_g, w_main, w_ab, a_log, dt_bias, *, tm):
    n = h2d.shape[0]
    row = lambda i: (i, 0)
    widths = [MIX_W, MIX_W, MIX_W, MIX_W, CONV_CH, MIX_W, LANE]
    dtypes = [BF16, F32, F32, F32, F32, F32, F32]
    return pl.pallas_call(
        _odd_in_body,
        grid=(n // tm,),
        in_specs=[pl.BlockSpec((tm, D_MODEL), row), _full((1, D_MODEL)),
                  _full(w_main.shape), _full(w_ab.shape), _full((1, LANE)), _full((1, LANE))],
        out_specs=[pl.BlockSpec((tm, w), row) for w in widths],
        out_shape=[jax.ShapeDtypeStruct((n, w), d) for w, d in zip(widths, dtypes)],
        compiler_params=_cparams(("parallel",)),
        name="odd_in",
    )(h2d, norm_g, w_main, w_ab, a_log, dt_bias)


def _sb_block(q_ref, k, v, acc_ref, c_ref, mask):
    tk = k.shape[0]
    jj = lax.broadcasted_iota(jnp.int32, (tk, tk), 0)
    ss = lax.broadcasted_iota(jnp.int32, (tk, tk), 1)
    later = (jj > ss).astype(BF16)
    k_bf = k.astype(BF16)
    v_bf = v.astype(BF16)
    for h in range(SB_HEADS):
        cols = slice(h * SB_DIM, (h + 1) * SB_DIM)
        z = _dot_nt(q_ref[:, cols], k_bf[:, cols])
        l1p = jnp.log1p(jnp.exp(-jnp.abs(z)))
        log_beta = jnp.minimum(z, 0.0) - l1p
        log_keep = -jnp.maximum(z, 0.0) - l1p
        if mask is not None:
            log_keep = jnp.where(mask, log_keep, 0.0)
        hi = log_keep.astype(BF16)
        lo = (log_keep - hi.astype(F32)).astype(BF16)
        c = c_ref[h][:, :1]
        after = _dot(hi, later) + _dot(lo, later) + c
        w = jnp.exp(log_beta + after)
        if mask is not None:
            w = jnp.where(mask, w, 0.0)
        acc_ref[:, cols] += _dot(w.astype(BF16), v_bf[:, cols])
        c_ref[h] = c_ref[h] + jnp.sum(log_keep, axis=1, keepdims=True)


def _sb_body(q_ref, kd_ref, vd_ref, k1_ref, v1_ref, k2_ref, v2_ref, szc_ref, kold_ref, vold_ref,
             out_ref, acc_ref, c_ref, kbuf, vbuf, sem, *, top_of):
    b = pl.program_id(0)
    i = pl.program_id(1)
    tq = q_ref.shape[0]
    tk = kbuf.shape[0]
    acc_ref[...] = jnp.zeros_like(acc_ref)
    c_ref[...] = jnp.zeros_like(c_ref)
    tt = lax.broadcasted_iota(jnp.int32, (tq, tq), 0)
    ss = lax.broadcasted_iota(jnp.int32, (tq, tq), 1)
    _sb_block(q_ref, kd_ref[...], vd_ref[...], acc_ref, c_ref, ss < tt)
    top = top_of(i)

    @pl.when(top >= 0)
    def _():
        _sb_block(q_ref, k1_ref[...], v1_ref[...], acc_ref, c_ref, None)

    @pl.when(top >= 1)
    def _():
        _sb_block(q_ref, k2_ref[...], v2_ref[...], acc_ref, c_ref, None)

    def alive():
        return jnp.max(c_ref[...]) > SB_DEAD

    def cond(carry):
        j, go = carry
        return jnp.logical_and(j >= 0, go)

    def body(carry):
        j, _ = carry
        r0 = pl.multiple_of(j * tk, tk)
        ck = pltpu.make_async_copy(kold_ref.at[b, pl.ds(r0, tk), :], kbuf, sem.at[0])
        cv = pltpu.make_async_copy(vold_ref.at[b, pl.ds(r0, tk), :], vbuf, sem.at[1])
        ck.start()
        cv.start()
        ck.wait()
        cv.wait()
        _sb_block(q_ref, kbuf[...], vbuf[...], acc_ref, c_ref, None)
        return j - 1, alive()

    lax.while_loop(cond, body, (top - 2, alive()))
    out_ref[...] = (acc_ref[...] * szc_ref[...]).astype(out_ref.dtype)


def _sb_attention(q, k_new, v_new, szc, k_past=None, v_past=None):
    bsz, length, _ = q.shape
    tq = min(length, SB_BLOCK)
    nq = length // tq
    tk = SB_BLOCK
    if k_past is None:
        assert tq == tk
        k_old, v_old = k_new, v_new
        top_of = lambda i: i - 1
    else:
        assert nq == 1 and k_past.shape[1] % tk == 0
        k_old, v_old = k_past, v_past
        n_old = k_past.shape[1] // tk
        top_of = lambda i: i * 0 + (n_old - 1)
    cur = pl.BlockSpec((None, tq, MIX_W), lambda b, i: (b, i, 0))
    old1 = pl.BlockSpec((None, tk, MIX_W), lambda b, i: (b, jnp.maximum(top_of(i), 0), 0))
    old2 = pl.BlockSpec((None, tk, MIX_W), lambda b, i: (b, jnp.maximum(top_of(i) - 1, 0), 0))
    anyspec = pl.BlockSpec(memory_space=pl.ANY)
    return pl.pallas_call(
        functools.partial(_sb_body, top_of=top_of),
        grid=(bsz, nq),
        in_specs=[cur, cur, cur, old1, old1, old2, old2, cur, anyspec, anyspec],
        out_specs=cur,
        out_shape=jax.ShapeDtypeStruct((bsz, length, MIX_W), BF16),
        scratch_shapes=[pltpu.VMEM((tq, MIX_W), F32), pltpu.VMEM((SB_HEADS, tq, LANE), F32),
                        pltpu.VMEM((tk, MIX_W), F32), pltpu.VMEM((tk, MIX_W), F32),
                        pltpu.SemaphoreType.DMA((2,))],
        compiler_params=_cparams(("parallel", "parallel")),
        name="sb_attention",
    )(q, k_new, v_new, k_old, v_old, k_old, v_old, szc, k_old, v_old)


def _dot01(m01, x):
    hi, mid, lo = _split3(x)
    return _dot(m01, hi) + _dot(m01, mid) + _dot(m01, lo)


def _dot01_r(x, m01):
    hi, mid, lo = _split3(x)
    return _dot(hi, m01) + _dot(mid, m01) + _dot(lo, m01)


def _gdn_prep_body(cur_ref, prev_ref, buf_ref, cw_ref, gb_ref,
                   w_ref, u_ref, kd_ref, q_ref, aqk_ref, eg_ref):
    c = GDN_CHUNK
    i = pl.program_id(1)
    cur = cur_ref[...]
    prev = jnp.where(i == 0, buf_ref[...], prev_ref[...])
    xp = jnp.concatenate([prev, cur], axis=0)
    off = SUBLANE - (CONV_W - 1)
    y = xp[off:off + c] * cw_ref[0:1, :]
    for t in range(1, CONV_W):
        y = y + xp[off + t:off + t + c] * cw_ref[t:t + 1, :]
    y = _silu(y)

    gb = gb_ref[...]
    rr = lax.broadcasted_iota(jnp.int32, (c, c), 0)
    cc = lax.broadcasted_iota(jnp.int32, (c, c), 1)
    causal = cc <= rr
    strict = cc < rr
    g_col = _dot01(causal.astype(BF16), gb)
    g_row = _dot01_r(gb.T[:SUBLANE, :], (rr <= cc).astype(BF16))
    eye = (rr == cc).astype(F32)
    hw = GDN_DIM
    for h in range(GDN_HEADS):
        qh = y[:, h * hw:(h + 1) * hw]
        kh = y[:, MIX_W + h * hw:MIX_W + (h + 1) * hw]
        vh = y[:, 2 * MIX_W + h * hw:2 * MIX_W + (h + 1) * hw]
        qh = qh * lax.rsqrt(jnp.sum(qh * qh, axis=-1, keepdims=True) + EPS) * (hw ** -0.5)
        kh = kh * lax.rsqrt(jnp.sum(kh * kh, axis=-1, keepdims=True) + EPS)
        q_bf, k_bf = qh.astype(BF16), kh.astype(BF16)
        gc = g_col[:, h:h + 1]
        gr = g_row[h:h + 1, :]
        beta = gb[:, GDN_HEADS + h:GDN_HEADS + h + 1]
        decay = jnp.exp(jnp.where(causal, gc - gr, -1e30))
        kk = _dot_nt(k_bf, k_bf)
        nmat = jnp.where(strict, beta * decay * kk, 0.0)
        tinv = eye - nmat
        pw = nmat
        for _ in range(int(math.log2(c)) - 1):
            pw = jnp.dot(pw, pw, precision=HI, preferred_element_type=F32)
            tinv = tinv + jnp.dot(tinv, pw, precision=HI, preferred_element_type=F32)
        eg = jnp.exp(gc)
        w = jnp.dot(tinv, beta * eg * kh, precision=HI, preferred_element_type=F32)
        u = jnp.dot(tinv, beta * vh, precision=HI, preferred_element_type=F32)
        aqk = jnp.where(causal, _dot_nt(q_bf, k_bf) * decay, 0.0)
        g_last = gc[c - 1:c, :]
        cols = slice(h * hw, (h + 1) * hw)
        w_ref[:, cols] = w.astype(w_ref.dtype)
        u_ref[:, cols] = u
        kd_ref[:, cols] = (kh * jnp.exp(g_last - gc)).astype(kd_ref.dtype)
        q_ref[:, cols] = q_bf
        aqk_ref[:, h * c:(h + 1) * c] = aqk.astype(aqk_ref.dtype)
    eg_ref[...] = jnp.exp(g_col)


def _gdn_prep(qkv, conv_buf8, conv_w, gb):
    bsz, length, _ = qkv.shape
    c = GDN_CHUNK
    nc = length // c
    per = c // SUBLANE
    cur = lambda w: pl.BlockSpec((None, c, w), lambda b, i: (b, i, 0))
    prev = pl.BlockSpec((None, SUBLANE, CONV_CH), lambda b, i: (b, jnp.maximum(i * per - 1, 0), 0))
    buf = pl.BlockSpec((None, SUBLANE, CONV_CH), lambda b, i: (b, 0, 0))
    shapes = [(MIX_W, BF16), (MIX_W, F32), (MIX_W, BF16), (MIX_W, BF16),
              (GDN_HEADS * c, BF16), (LANE, F32)]
    return pl.pallas_call(
        _gdn_prep_body,
        grid=(bsz, nc),
        in_specs=[cur(CONV_CH), prev, buf, _full(conv_w.shape), cur(LANE)],
        out_specs=[cur(w) for w, _ in shapes],
        out_shape=[jax.ShapeDtypeStruct((bsz, length, w), d) for w, d in shapes],
        compiler_params=_cparams(("parallel", "parallel")),
        name="gdn_prep",
    )(qkv, qkv, conv_buf8, conv_w, gb)


def _gdn_seq_body(w_ref, u_ref, kd_ref, q_ref, aqk_ref, eg_ref, szd_ref, s0_ref, ng_ref,
                  out_ref, s_ref):
    c = GDN_CHUNK
    i = pl.program_id(1)

    @pl.when(i == 0)
    def _():
        s_ref[...] = s0_ref[...]

    hw = GDN_DIM
    for h in range(GDN_HEADS):
        cols = slice(h * hw, (h + 1) * hw)
        s = s_ref[h]
        s_bf = s.astype(BF16)
        delta = u_ref[:, cols] - _dot(w_ref[:, cols], s_bf)
        d_bf = delta.astype(BF16)
        eg = eg_ref[:, h:h + 1]
        o = eg * _dot(q_ref[:, cols], s_bf) + _dot(aqk_ref[:, h * c:(h + 1) * c], d_bf)
        s_ref[h] = eg[c - 1:c, :] * s + _dot_tn(kd_ref[:, cols], d_bf)
        out_ref[:, cols] = (_rms(o, ng_ref[...]) * szd_ref[:, cols]).astype(out_ref.dtype)


def _gdn_seq(w, u, kd, q, aqk, eg, szd, s0, norm_g):
    bsz, length, _ = w.shape
    c = GDN_CHUNK
    cur = lambda width: pl.BlockSpec((None, c, width), lambda b, i: (b, i, 0))
    st = pl.BlockSpec((None, GDN_HEADS, GDN_DIM, GDN_DIM), lambda b, i: (b, 0, 0, 0))
    return pl.pallas_call(
        _gdn_seq_body,
        grid=(bsz, length // c),
        in_specs=[cur(MIX_W), cur(MIX_W), cur(MIX_W), cur(MIX_W), cur(GDN_HEADS * c), cur(LANE),
                  cur(MIX_W), st, _full((1, GDN_DIM))],
        out_specs=[cur(MIX_W), st],
        out_shape=[jax.ShapeDtypeStruct((bsz, length, MIX_W), BF16),
                   jax.ShapeDtypeStruct(s0.shape, F32)],
        compiler_params=_cparams(("parallel", "arbitrary")),
        name="gdn_seq",
    )(w, u, kd, q, aqk, eg, szd, s0, norm_g)


def _row_tile(n):
    for tm in (512, 256, 128, 64):
        if n % tm == 0:
            return tm
    raise ValueError(n)


def _trunk(x, p, past, wts, s5w):
    bsz, length, _ = x.shape
    n = bsz * length
    x2d = x.reshape(n, D_MODEL)
    first = past is None
    chunk = min(length, MLP_CHUNK)
    time_major = length >= 512
    nb = bsz if time_major else 1
    tm = _row_tile(n // nb)

    outs = _even_in(x2d, wts["norm_g0"], wts["even_w_in"], wts["a_ln_g"], wts["a_ln_b"],
                    wts["a_w_s"][:, :chunk, :chunk], wts["a_b_s"][:, :chunk, :],
                    nb=nb, chunk=chunk, tm=tm, want_av=not first)
    a_out, ub, szb = outs[:3]
    a_v = None if first else outs[3].reshape(1, bsz, length, MIX_W)
    if time_major:
        ub_tm = ub.reshape(length, bsz, MIX_W)
        szb_tm = szb.reshape(length, bsz, MIX_W)
    else:
        ub_tm = ub.reshape(bsz, length, MIX_W).transpose(1, 0, 2)
        szb_tm = szb.reshape(bsz, length, MIX_W).transpose(1, 0, 2)
    if first:
        x0r = jnp.zeros((bsz, S5_LANES), F32)
        x0i = x0r
    else:
        x0r = past["b_re"].reshape(bsz, S5_LANES)
        x0i = past["b_im"].reshape(bsz, S5_LANES)
    b_out_tm, xr, xi = _s5(ub_tm, szb_tm, x0r, x0i, *s5w, wts["b_D"], wts["b_glu_w"],
                           t_steps=min(length, 64))
    if time_major:
        b_out = b_out_tm.reshape(length, bsz * MIX_W)
    else:
        b_out = b_out_tm.transpose(1, 0, 2).reshape(n, MIX_W)
    p0 = p[0].reshape(n, PLE_DIM)
    h = _mix_out(a_out, b_out, x2d, p0, wts["even_w_out"], wts["ple_norm_g0"], wts["ple_gate_w0"],
                 wts["ple_proj0"], None, nb=nb, tm=tm, b_time_major=time_major)

    tm1 = _row_tile(n)
    q, k, v, szc, qkv, szd, gb = _odd_in(h, wts["norm_g1"], wts["odd_w_main"], wts["odd_w_ab"],
                                         wts["d_A_log"], wts["d_dt_bias"], tm=min(tm1, 256))
    to3 = lambda a: a.reshape(bsz, length, a.shape[-1])
    q, k, v, szc, qkv, szd, gb = map(to3, (q, k, v, szc, qkv, szd, gb))
    if first:
        c_out = _sb_attention(q, k, v, szc)
        buf8 = jnp.zeros((bsz, SUBLANE, CONV_CH), F32)
        s0 = jnp.zeros((bsz, GDN_HEADS, GDN_DIM, GDN_DIM), F32)
    else:
        c_out = _sb_attention(q, k, v, szc, past["k_c"], past["v_c"])
        buf8 = jnp.pad(past["conv_d"], ((0, 0), (SUBLANE - (CONV_W - 1), 0), (0, 0)))
        s0 = past["s_d"]
    w_, u_, kd_, qd_, aqk_, eg_ = _gdn_prep(qkv, buf8, wts["d_conv_w"], gb)
    d_out, s_new = _gdn_seq(w_, u_, kd_, qd_, aqk_, eg_, szd, s0, wts["d_norm_g"])
    p1 = p[1].reshape(n, PLE_DIM)
    y = _mix_out(c_out.reshape(n, MIX_W), d_out.reshape(n, MIX_W), h, p1, wts["odd_w_out"],
                 wts["ple_norm_g1"], wts["ple_gate_w1"], wts["ple_proj1"], wts["final_norm_g"],
                 nb=1, tm=tm1, b_time_major=False)

    conv_new = qkv[:, length - (CONV_W - 1):, :]
    return (y.reshape(bsz, length, D_MODEL),
            xr.reshape(1, bsz, S5_GROUPS, S5_N), xi.reshape(1, bsz, S5_GROUPS, S5_N), a_v,
            k.reshape(1, bsz, length, SB_HEADS, SB_DIM), v.reshape(1, bsz, length, SB_HEADS, SB_DIM),
            s_new[None], conv_new[None])


def kernel(x_prompt, x_sample, state_b_re, state_b_im, cache_k_c, cache_v_c, state_d, state_conv_d,
           p_prompt, p_sample,
           norm_g, final_norm_g, ple_proj, ple_gate_w, ple_norm_g,
           even_w_in, even_w_out, a_ln_g, a_ln_b, a_w_s, a_b_s,
           b_lam_re, b_lam_im, b_log_dt, b_B_re, b_B_im, b_C_re, b_C_im, b_D, b_glu_w,
           odd_w_in, odd_w_out, d_conv_w, d_A_log, d_dt_bias, d_norm_g):
    row = lambda a: a.reshape(1, -1).astype(F32)
    tril = jnp.tril(jnp.ones((MLP_CHUNK, MLP_CHUNK), dtype=bool))
    n_main = 8 * MIX_W
    pad_lane = lambda a: jnp.pad(a.reshape(1, -1).astype(F32), ((0, 0), (0, LANE - a.size)))
    wts = {
        "norm_g0": row(norm_g[0]), "norm_g1": row(norm_g[1]), "final_norm_g": row(final_norm_g),
        "ple_norm_g0": row(ple_norm_g[0]), "ple_norm_g1": row(ple_norm_g[1]),
        "ple_gate_w0": ple_gate_w[0].astype(BF16), "ple_gate_w1": ple_gate_w[1].astype(BF16),
        "ple_proj0": ple_proj[0].astype(BF16), "ple_proj1": ple_proj[1].astype(BF16),
        "even_w_in": even_w_in[0].astype(BF16), "even_w_out": even_w_out[0].astype(BF16),
        "a_ln_g": row(a_ln_g[0]), "a_ln_b": row(a_ln_b[0]),
        "a_w_s": jnp.where(tril, a_w_s[0], 0.0).astype(BF16),
        "a_b_s": jnp.broadcast_to(a_b_s[0][:, :, None], (A_GROUPS, MLP_CHUNK, LANE)).astype(F32),
        "b_D": row(b_D[0]), "b_glu_w": b_glu_w[0].astype(BF16),
        "odd_w_main": odd_w_in[0][:, :n_main].astype(BF16),
        "odd_w_ab": jnp.pad(odd_w_in[0][:, n_main:], ((0, 0), (0, LANE - 2 * GDN_HEADS))).astype(BF16),
        "odd_w_out": odd_w_out[0].astype(BF16),
        "d_conv_w": d_conv_w[0].astype(F32),
        "d_A_log": pad_lane(d_A_log[0]), "d_dt_bias": pad_lane(d_dt_bias[0]),
        "d_norm_g": row(d_norm_g[0]),
    }
    ar, ai, bbr, bbi = _s5_disc(b_lam_re[0], b_lam_im[0], b_log_dt[0], b_B_re[0], b_B_im[0])
    bre, bim, cre, cim = _s5_block_diag((bbr, bbi), b_C_re[0], b_C_im[0])
    s5w = (ar.reshape(1, S5_LANES), ai.reshape(1, S5_LANES), bre, bim, cre, cim)

    (y_p, b_re_p, b_im_p, _, k_p, v_p, s_p, conv_p) = _trunk(x_prompt, p_prompt, None, wts, s5w)
    dec_b, past_len = cache_k_c.shape[1], cache_k_c.shape[2]
    past = {"b_re": state_b_re[0], "b_im": state_b_im[0],
            "k_c": cache_k_c[0].reshape(dec_b, past_len, MIX_W),
            "v_c": cache_v_c[0].reshape(dec_b, past_len, MIX_W),
            "s_d": state_d[0], "conv_d": state_conv_d[0]}
    (y_s, b_re_s, b_im_s, a_v_s, k_s, v_s, s_s, conv_s) = _trunk(x_sample, p_sample, past, wts, s5w)
    return (y_p, y_s, b_re_p, b_im_p, k_p, v_p, s_p, conv_p,
            b_re_s, b_im_s, a_v_s, k_s, v_s, s_s, conv_s)
```

```python
import functools
import math

import jax
import jax.numpy as jnp
from jax import lax
from jax.experimental import pallas as pl
from jax.experimental.pallas import tpu as pltpu

F32 = jnp.float32
BF16 = jnp.bfloat16
EPS = 1e-6

D_MODEL = 1024
PLE_DIM = 256
MIX_W = 512
A_GROUPS = 4
MLP_CHUNK = 128
S5_GROUPS = 32
S5_P = 16
S5_N = 64
S5_LANES = S5_GROUPS * S5_N
SB_HEADS = 8
SB_DIM = 64
SB_BLOCK = 128
GDN_HEADS = 4
GDN_DIM = 128
GDN_CHUNK = 64
CONV_W = 4
CONV_CH = 3 * MIX_W
LANE = 128
SUBLANE = 8
VMEM_LIMIT = 52 * 1024 * 1024
SB_DEAD = -104.0


def _cparams(sem):
    return pltpu.CompilerParams(dimension_semantics=sem, vmem_limit_bytes=VMEM_LIMIT)


def _gelu(x):
    return 0.5 * x * (1.0 + jnp.tanh(0.7978845608028654 * (x + 0.044715 * (x * x * x))))


def _sigmoid(x):
    return 1.0 / (1.0 + jnp.exp(-x))


def _silu(x):
    return x * _sigmoid(x)


def _softplus(x):
    return jnp.maximum(x, 0.0) + jnp.log1p(jnp.exp(-jnp.abs(x)))


def _rms(x, g):
    ms = jnp.mean(x * x, axis=-1, keepdims=True)
    return x * lax.rsqrt(ms + EPS) * g


def _dot(a, b):
    return jnp.dot(a, b, preferred_element_type=F32)


def _dot_nt(a, b):
    return lax.dot_general(a, b, (((1,), (1,)), ((), ())), preferred_element_type=F32)


def _dot_tn(a, b):
    return lax.dot_general(a, b, (((0,), (0,)), ((), ())), preferred_element_type=F32)


def _split3(x):
    hi = x.astype(BF16)
    r = x - hi.astype(F32)
    mid = r.astype(BF16)
    lo = (r - mid.astype(F32)).astype(BF16)
    return hi, mid, lo


def _full(shape):
    n = len(shape)
    return pl.BlockSpec(shape, lambda *_: (0,) * n)


def _even_in_body(x_ref, g_ref, w_ref, lng_ref, lnb_ref, ws_ref, bs_ref,
                  aout_ref, ub_ref, szb_ref, av_ref, *, chunk):
    tm = x_ref.shape[0]
    hn = _rms(x_ref[...], g_ref[...]).astype(BF16)

    def proj(k):
        return _dot(hn, w_ref[:, k * MIX_W:(k + 1) * MIX_W])

    va = _gelu(proj(1))
    mu = jnp.mean(va, axis=-1, keepdims=True)
    vc = va - mu
    var = jnp.mean(vc * vc, axis=-1, keepdims=True)
    va = vc * lax.rsqrt(var + EPS) * lng_ref[...] + lnb_ref[...]
    if av_ref is not None:
        av_ref[...] = va
    va_bf = va.astype(BF16)
    gate = _gelu(proj(0)) * _silu(proj(2))
    gw = MIX_W // A_GROUPS
    for ch in range(tm // chunk):
        rows = slice(ch * chunk, (ch + 1) * chunk)
        for g in range(A_GROUPS):
            cols = slice(g * gw, (g + 1) * gw)
            s = _dot(ws_ref[g], va_bf[rows, cols]) + bs_ref[g]
            aout_ref[rows, cols] = (gate[rows, cols] * s).astype(aout_ref.dtype)
    ub_ref[...] = proj(3)
    szb_ref[...] = _silu(proj(4))


def _even_in(x2d, norm_g, w_in, ln_g, ln_b, w_s, b_s, *, nb, chunk, tm, want_av):
    n = x2d.shape[0]
    nt = n // (nb * tm)
    body = functools.partial(_even_in_body, chunk=chunk)
    if not want_av:
        body_fn = lambda *r: body(*r, None)
    else:
        body_fn = body
    row = lambda b, i: (b * nt + i, 0)
    tmaj = lambda b, i: (i, b)
    out_shape = [jax.ShapeDtypeStruct((n, MIX_W), BF16),
                 jax.ShapeDtypeStruct((nt * tm, nb * MIX_W), F32),
                 jax.ShapeDtypeStruct((nt * tm, nb * MIX_W), F32)]
    out_specs = [pl.BlockSpec((tm, MIX_W), row),
                 pl.BlockSpec((tm, MIX_W), tmaj),
                 pl.BlockSpec((tm, MIX_W), tmaj)]
    if want_av:
        out_shape.append(jax.ShapeDtypeStruct((n, MIX_W), F32))
        out_specs.append(pl.BlockSpec((tm, MIX_W), row))
    return pl.pallas_call(
        body_fn,
        grid=(nb, nt),
        in_specs=[pl.BlockSpec((tm, D_MODEL), row),
                  _full((1, D_MODEL)),
                  _full(w_in.shape),
                  _full((1, MIX_W)), _full((1, MIX_W)),
                  _full(w_s.shape), _full(b_s.shape)],
        out_specs=out_specs,
        out_shape=out_shape,
        compiler_params=_cparams(("parallel", "parallel")),
        name="even_in",
    )(x2d, norm_g, w_in, ln_g, ln_b, w_s, b_s)


def _s5_disc_body(lr_ref, li_ref, ldt_ref, br_ref, bi_ref, ar_ref, ai_ref, bbr_ref, bbi_ref):
    dt = jnp.exp(ldt_ref[...])
    lr, li = lr_ref[...], li_ref[...]
    mag = jnp.exp(lr * dt)
    ar = mag * jnp.cos(li * dt)
    ai = mag * jnp.sin(li * dt)
    den = lr * lr + li * li
    cr = ((ar - 1.0) * lr + ai * li) / den
    ci = (ai * lr - (ar - 1.0) * li) / den
    ar_ref[...] = ar
    ai_ref[...] = ai
    for p in range(S5_P):
        br, bi = br_ref[p], bi_ref[p]
        bbr_ref[p] = cr * br - ci * bi
        bbi_ref[p] = cr * bi + ci * br


def _s5_disc(lam_re, lam_im, log_dt, b_re, b_im):
    gn = (S5_GROUPS, S5_N)
    pgn = (S5_P, S5_GROUPS, S5_N)
    return pl.pallas_call(
        _s5_disc_body,
        out_shape=[jax.ShapeDtypeStruct(gn, F32), jax.ShapeDtypeStruct(gn, F32),
                   jax.ShapeDtypeStruct(pgn, F32), jax.ShapeDtypeStruct(pgn, F32)],
        name="s5_disc",
    )(lam_re, lam_im, log_dt.reshape(S5_GROUPS, 1),
      jnp.transpose(b_re, (2, 0, 1)), jnp.transpose(b_im, (2, 0, 1)))


def _s5_block_diag(bb_pgn, c_re, c_im):
    gpc = LANE // S5_P
    nj = S5_GROUPS // gpc
    eye = jnp.eye(gpc, dtype=F32)

    def b_blk(bb):
        b = jnp.transpose(bb, (1, 0, 2)).reshape(nj, gpc, S5_P, S5_N)
        m = b[:, :, :, None, :] * eye[None, :, None, :, None]
        return m.reshape(nj, gpc * S5_P, gpc * S5_N).astype(BF16)

    def c_blk(c):
        cc = jnp.transpose(c, (0, 2, 1)).reshape(nj, gpc, S5_N, S5_P)
        m = cc[:, :, :, None, :] * eye[None, :, None, :, None]
        return m.reshape(nj, gpc * S5_N, gpc * S5_P).astype(BF16)

    return b_blk(bb_pgn[0]), b_blk(bb_pgn[1]), c_blk(c_re), c_blk(-c_im)


def _s5_body(u_ref, szb_ref, x0r_ref, x0i_ref, ar_ref, ai_ref, bre_ref, bim_ref, cre_ref, cim_ref,
             d_ref, glu_ref, out_ref, st_r, st_i, sre, sim):
    t_steps = u_ref.shape[0]
    rows = t_steps * SUBLANE
    ti = pl.program_id(1)
    cw = bre_ref.shape[2]
    nj = bre_ref.shape[0]

    @pl.when(ti == 0)
    def _():
        st_r[...] = x0r_ref[...]
        st_i[...] = x0i_ref[...]

    u = u_ref[...].reshape(rows, MIX_W)
    u_bf = u.astype(BF16)
    for j in range(nj):
        uj = u_bf[:, j * LANE:(j + 1) * LANE]
        sre[:, j * cw:(j + 1) * cw] = _dot(uj, bre_ref[j])
        sim[:, j * cw:(j + 1) * cw] = _dot(uj, bim_ref[j])

    for j in range(nj):
        lanes = slice(j * cw, (j + 1) * cw)
        a_r = jnp.broadcast_to(ar_ref[:, lanes], (SUBLANE, cw))
        a_i = jnp.broadcast_to(ai_ref[:, lanes], (SUBLANE, cw))

        def step(t, carry, lanes=lanes, a_r=a_r, a_i=a_i):
            xr, xi = carry
            r0 = pl.multiple_of(t * SUBLANE, SUBLANE)
            nxr = a_r * xr - a_i * xi + sre[pl.ds(r0, SUBLANE), lanes]
            nxi = a_r * xi + a_i * xr + sim[pl.ds(r0, SUBLANE), lanes]
            sre[pl.ds(r0, SUBLANE), lanes] = nxr
            sim[pl.ds(r0, SUBLANE), lanes] = nxi
            return nxr, nxi

        xr, xi = lax.fori_loop(0, t_steps, step, (st_r[:, lanes], st_i[:, lanes]), unroll=8)
        st_r[:, lanes] = xr
        st_i[:, lanes] = xi

    ys = []
    for j in range(nj):
        lanes = slice(j * cw, (j + 1) * cw)
        ys.append(_dot(sre[:, lanes].astype(BF16), cre_ref[j])
                  + _dot(sim[:, lanes].astype(BF16), cim_ref[j]))
    y = jnp.concatenate(ys, axis=1) + d_ref[...] * u
    y = _gelu(y)
    y = y * _sigmoid(_dot(y.astype(BF16), glu_ref[...]))
    y = y * szb_ref[...].reshape(rows, MIX_W)
    out_ref[...] = y.reshape(out_ref.shape)


def _s5(u_tm, szb_tm, x0r, x0i, ar, ai, bre, bim, cre, cim, d_skip, glu_w, *, t_steps):
    length, nb, _ = u_tm.shape
    blk = pl.BlockSpec((t_steps, SUBLANE, MIX_W), lambda b, t: (t, b, 0))
    st = pl.BlockSpec((SUBLANE, S5_LANES), lambda b, t: (b, 0))
    rows = t_steps * SUBLANE
    return pl.pallas_call(
        _s5_body,
        grid=(nb // SUBLANE, length // t_steps),
        in_specs=[blk, blk, st, st, _full((1, S5_LANES)), _full((1, S5_LANES)),
                  _full(bre.shape), _full(bim.shape), _full(cre.shape), _full(cim.shape),
                  _full((1, MIX_W)), _full(glu_w.shape)],
        out_specs=[blk, st, st],
        out_shape=[jax.ShapeDtypeStruct((length, nb, MIX_W), F32),
                   jax.ShapeDtypeStruct((nb, S5_LANES), F32),
                   jax.ShapeDtypeStruct((nb, S5_LANES), F32)],
        scratch_shapes=[pltpu.VMEM((rows, S5_LANES), F32), pltpu.VMEM((rows, S5_LANES), F32)],
        compiler_params=_cparams(("parallel", "arbitrary")),
        name="s5_scan",
    )(u_tm, szb_tm, x0r, x0i, ar, ai, bre, bim, cre, cim, d_skip, glu_w)


def _mix_out_body(a_ref, b_ref, h_ref, p_ref, wout_ref, pg_ref, wgate_ref, wp_ref, fg_ref, out_ref):
    mix = (_dot(a_ref[...].astype(BF16), wout_ref[:MIX_W, :])
           + _dot(b_ref[...].astype(BF16), wout_ref[MIX_W:, :]))
    h1 = h_ref[...] + mix
    gate = _sigmoid(_dot(_rms(h1, pg_ref[...]).astype(BF16), wgate_ref[...]))
    h2 = h1 + gate * _dot(p_ref[...].astype(BF16), wp_ref[...])
    if fg_ref is not None:
        h2 = _rms(h2, fg_ref[...])
    out_ref[...] = h2


def _mix_out(a2d, b2d, h2d, p3d, layer, w_out, ple_g, w_gate, w_p, final_g, *, nb, tm,
             b_time_major):
    n = h2d.shape[0]
    nt = n // (nb * tm)
    row = lambda b, i: (b * nt + i, 0)
    tmaj = lambda b, i: (i, b)
    args = [a2d, b2d, h2d, p3d, w_out, ple_g, w_gate, w_p]
    in_specs = [pl.BlockSpec((tm, MIX_W), row),
                pl.BlockSpec((tm, MIX_W), tmaj if b_time_major else row),
                pl.BlockSpec((tm, D_MODEL), row),
                pl.BlockSpec((None, tm, PLE_DIM), lambda b, i: (layer, b * nt + i, 0)),
                _full(w_out.shape), _full((1, D_MODEL)), _full(w_gate.shape), _full(w_p.shape)]
    if final_g is None:
        body = lambda *r: _mix_out_body(*r[:8], None, r[8])
    else:
        body = _mix_out_body
        args.append(final_g)
        in_specs.append(_full((1, D_MODEL)))
    return pl.pallas_call(
        body,
        grid=(nb, nt),
        in_specs=in_specs,
        out_specs=pl.BlockSpec((tm, D_MODEL), row),
        out_shape=jax.ShapeDtypeStruct((n, D_MODEL), F32),
        compiler_params=_cparams(("parallel", "parallel")),
        name="mix_out",
    )(*args)


def _odd_in_body(h_ref, g_ref, w_ref, wab_ref, alog_ref, dtb_ref,
                 q_ref, k_ref, v_ref, szc_ref, qkv_ref, szd_ref, gb_ref):
    hn = _rms(h_ref[...], g_ref[...]).astype(BF16)

    def proj(k, width=MIX_W):
        return _dot(hn, w_ref[:, k * MIX_W:k * MIX_W + width])

    q_ref[...] = (proj(0) * (SB_DIM ** -0.5)).astype(BF16)
    k_ref[...] = proj(1)
    v_ref[...] = proj(2)
    szc_ref[...] = _silu(proj(3))
    for k in range(3):
        qkv_ref[:, k * MIX_W:(k + 1) * MIX_W] = proj(4 + k)
    szd_ref[...] = _silu(proj(7))
    ab = _dot(hn, wab_ref[...])
    g = -jnp.exp(alog_ref[...]) * _softplus(ab + dtb_ref[...])
    lane = lax.broadcasted_iota(jnp.int32, ab.shape, 1)
    gb_ref[...] = jnp.where(lane < GDN_HEADS, g, _sigmoid(ab))


def _odd_in(h2d, norm_g, w_main, w_ab, a_log, dt_bias, *, tm):
    n = h2d.shape[0]
    row = lambda i: (i, 0)
    widths = [MIX_W, MIX_W, MIX_W, MIX_W, CONV_CH, MIX_W, LANE]
    dtypes = [BF16, F32, F32, F32, F32, F32, F32]
    return pl.pallas_call(
        _odd_in_body,
        grid=(n // tm,),
        in_specs=[pl.BlockSpec((tm, D_MODEL), row), _full((1, D_MODEL)),
                  _full(w_main.shape), _full(w_ab.shape), _full((1, LANE)), _full((1, LANE))],
        out_specs=[pl.BlockSpec((tm, w), row) for w in widths],
        out_shape=[jax.ShapeDtypeStruct((n, w), d) for w, d in zip(widths, dtypes)],
        compiler_params=_cparams(("parallel",)),
        name="odd_in",
    )(h2d, norm_g, w_main, w_ab, a_log, dt_bias)


def _sb_pairs(q2x, k_blocks, v_blocks, masks, valids, c_in, cum_rhs):
    npair = len(q2x)
    nblk = len(k_blocks[0])
    rows = q2x[0].shape[0]
    z = [_dot_nt(q2x[p], jnp.concatenate(k_blocks[p], axis=0)) for p in range(npair)]
    log_beta, hilo = [], []
    for p in range(npair):
        lb, hl = [], []
        for j in range(nblk):
            zj = z[p][:, j * LANE:(j + 1) * LANE]
            l1p = jnp.log(1.0 + jnp.exp(-jnp.abs(zj)))
            lb.append(jnp.minimum(zj, 0.0) - l1p)
            log_keep = -jnp.maximum(zj, 0.0) - l1p
            if masks[j] is not None:
                log_keep = jnp.where(masks[j], log_keep, 0.0)
            if valids[j] is not None:
                log_keep = jnp.where(valids[j], log_keep, 0.0)
            hi = log_keep.astype(BF16)
            lo = (log_keep - hi.astype(F32)).astype(BF16)
            hl.append(jnp.concatenate([hi, lo], axis=1))
        log_beta.append(lb)
        hilo.append(jnp.concatenate(hl, axis=0))
    cum = [_dot(x, cum_rhs) for x in hilo]
    ws, totals = [], []
    for p in range(npair):
        c = c_in[p]
        w_p = [None] * nblk
        for j in reversed(range(nblk)):
            after = cum[p][j * rows:(j + 1) * rows, :LANE]
            if c is not None:
                after = after + c
            w = jnp.exp(log_beta[p][j] + after)
            if masks[j] is not None:
                w = jnp.where(masks[j], w, 0.0)
            if valids[j] is not None:
                w = jnp.where(valids[j], w, 0.0)
            w_p[j] = w.astype(BF16)
            total = cum[p][j * rows:(j + 1) * rows, LANE:]
            c = total if c is None else c + total
        ws.append(jnp.concatenate(w_p, axis=1))
        totals.append(c)
    pv = [_dot(ws[p], jnp.concatenate(v_blocks[p], axis=0)) for p in range(npair)]
    return pv, totals


def _rows512(ref):
    if len(ref.shape) == 2:
        return ref[...]
    return jnp.concatenate([ref[:, h, :] for h in range(SB_HEADS)], axis=1)


def _sb_body(q_ref, kd_ref, vd_ref, k1_ref, v1_ref, k2_ref, v2_ref, szc_ref, cum_ref,
             kold_ref, vold_ref, out_ref, acc_ref, c_ref, kbuf, vbuf, sem, *, top_of):
    b = pl.program_id(0)
    i = pl.program_id(1)
    tq = q_ref.shape[0]
    tk = kbuf.shape[0]
    npair = MIX_W // LANE
    tt = lax.broadcasted_iota(jnp.int32, (2 * tq, tk), 0)
    ss = lax.broadcasted_iota(jnp.int32, (2 * tq, tk), 1)
    diag_mask = ss < jnp.where(tt >= tq, tt - tq, tt)
    even_lane = lax.broadcasted_iota(jnp.int32, (tq, LANE), 1) < SB_DIM
    top = top_of(i)

    def pad_rows(x):
        if x.shape[0] == tk:
            return x
        return jnp.concatenate([x, jnp.zeros((tk - x.shape[0], x.shape[1]), x.dtype)], axis=0)

    k_win = [_rows512(k2_ref), _rows512(k1_ref), pad_rows(kd_ref[...])]
    v_win = [_rows512(v2_ref), _rows512(v1_ref), pad_rows(vd_ref[...])]
    masks = [None, None, diag_mask]
    valids = [top >= 1, top >= 0, None]
    cum_rhs = cum_ref[...]

    def q_pair(p):
        q2 = q_ref[:, p * LANE:(p + 1) * LANE]
        zero = jnp.zeros_like(q2)
        return jnp.concatenate([jnp.where(even_lane, q2, zero), jnp.where(even_lane, zero, q2)], axis=0)

    def tiles(blocks, p):
        return [x[:, p * LANE:(p + 1) * LANE].astype(BF16) for x in blocks]

    def merge(pv):
        return jnp.where(even_lane, pv[:tq], pv[tq:])

    pairs = range(npair)
    pvs, sums = _sb_pairs([q_pair(p) for p in pairs], [tiles(k_win, p) for p in pairs],
                          [tiles(v_win, p) for p in pairs], masks, valids, [None] * npair, cum_rhs)
    acc_all = jnp.concatenate([merge(pv) for pv in pvs], axis=1)
    out_ref[...] = (acc_all * szc_ref[...]).astype(out_ref.dtype)
    c_max = functools.reduce(jnp.maximum, sums)

    @pl.when(jnp.logical_and(top >= 2, jnp.max(c_max) > SB_DEAD))
    def _():
        acc_ref[...] = acc_all
        for p in range(npair):
            c_ref[p] = sums[p]

        def cond(carry):
            j, go = carry
            return jnp.logical_and(j >= 0, go)

        def body(carry):
            j, _ = carry
            r0 = pl.multiple_of(j * tk, tk)
            ck = pltpu.make_async_copy(kold_ref.at[b, pl.ds(r0, tk)], kbuf, sem.at[0])
            cv = pltpu.make_async_copy(vold_ref.at[b, pl.ds(r0, tk)], vbuf, sem.at[1])
            ck.start()
            cv.start()
            ck.wait()
            cv.wait()
            k = [_rows512(kbuf)]
            v = [_rows512(vbuf)]
            pvs, cs = _sb_pairs([q_pair(p) for p in pairs], [tiles(k, p) for p in pairs],
                                [tiles(v, p) for p in pairs], [None], [None],
                                [c_ref[p] for p in pairs], cum_rhs)
            for p in pairs:
                acc_ref[:, p * LANE:(p + 1) * LANE] += merge(pvs[p])
                c_ref[p] = cs[p]
            return j - 1, jnp.max(c_ref[...]) > SB_DEAD

        lax.while_loop(cond, body, (top - 2, jnp.max(c_max) > SB_DEAD))
        out_ref[...] = (acc_ref[...] * szc_ref[...]).astype(out_ref.dtype)


def _sb_attention(q, k_new, v_new, szc, k_past=None, v_past=None):
    bsz, length, _ = q.shape
    tq = min(length, SB_BLOCK)
    nq = length // tq
    tk = SB_BLOCK
    if k_past is None:
        assert tq == tk
        k_old, v_old = k_new, v_new
        top_of = lambda i: i - 1
        old_blk = (tk, MIX_W)
    else:
        assert nq == 1 and k_past.shape[1] % tk == 0
        k_old, v_old = k_past, v_past
        n_old = k_past.shape[1] // tk
        top_of = lambda i: i * 0 + (n_old - 1)
        old_blk = (tk, SB_HEADS, SB_DIM)
    tail = (0,) * (len(old_blk) - 1)
    cur = pl.BlockSpec((None, tq, MIX_W), lambda b, i: (b, i, 0))
    old1 = pl.BlockSpec((None,) + old_blk, lambda b, i: (b, jnp.maximum(top_of(i), 0)) + tail)
    old2 = pl.BlockSpec((None,) + old_blk, lambda b, i: (b, jnp.maximum(top_of(i) - 1, 0)) + tail)
    anyspec = pl.BlockSpec(memory_space=pl.ANY)
    jj = lax.broadcasted_iota(jnp.int32, (tk, tk), 0)
    ss = lax.broadcasted_iota(jnp.int32, (tk, tk), 1)
    half = jnp.concatenate([(jj > ss).astype(BF16), jnp.ones((tk, LANE), BF16)], axis=1)
    cum_rhs = jnp.concatenate([half, half], axis=0)
    return pl.pallas_call(
        functools.partial(_sb_body, top_of=top_of),
        grid=(bsz, nq),
        in_specs=[cur, cur, cur, old1, old1, old2, old2, cur, _full(cum_rhs.shape),
                  anyspec, anyspec],
        out_specs=cur,
        out_shape=jax.ShapeDtypeStruct((bsz, length, MIX_W), BF16),
        scratch_shapes=[pltpu.VMEM((tq, MIX_W), F32),
                        pltpu.VMEM((MIX_W // LANE, 2 * tq, LANE), F32),
                        pltpu.VMEM(old_blk, F32), pltpu.VMEM(old_blk, F32),
                        pltpu.SemaphoreType.DMA((2,))],
        compiler_params=_cparams(("parallel", "parallel")),
        name="sb_attention",
    )(q, k_new, v_new, k_old, v_old, k_old, v_old, szc, cum_rhs, k_old, v_old)


def _dot01(m01, x):
    hi, mid, lo = _split3(x)
    return _dot(m01, hi) + _dot(m01, mid) + _dot(m01, lo)


def _dot01_r(x, m01):
    hi, mid, lo = _split3(x)
    return _dot(hi, m01) + _dot(mid, m01) + _dot(lo, m01)


def _gdn_prep_body(cur_ref, prev_ref, buf_ref, cw_ref, gb_ref, e64_ref, e128_ref, bd_ref,
                   w_ref, u_ref, kd_ref, q_ref, aqk_ref, eg_ref):
    c = GDN_CHUNK
    rows = cur_ref.shape[0]
    i = pl.program_id(1)
    cur = cur_ref[...]
    prev = jnp.where(i == 0, buf_ref[...], prev_ref[...])
    xp = jnp.concatenate([prev, cur], axis=0)
    off = SUBLANE - (CONV_W - 1)
    y = xp[off:off + rows] * cw_ref[0:1, :]
    for t in range(1, CONV_W):
        y = y + xp[off + t:off + t + rows] * cw_ref[t:t + 1, :]
    y = _silu(y)

    nh, hw = GDN_HEADS, GDN_DIM
    hc = nh * c
    nch = rows // c

    def l2n(x, scale):
        parts = []
        for h in range(nh):
            xh = x[:, h * hw:(h + 1) * hw]
            parts.append(xh * (lax.rsqrt(jnp.sum(xh * xh, axis=-1, keepdims=True) + EPS) * scale))
        return jnp.concatenate(parts, axis=1)

    q = l2n(y[:, :MIX_W], hw ** -0.5)
    k = l2n(y[:, MIX_W:2 * MIX_W], 1.0)
    v = y[:, 2 * MIX_W:]
    q_bf, k_bf = q.astype(BF16), k.astype(BF16)
    q_ref[...] = q_bf

    gb = gb_ref[...]
    r_i = lax.broadcasted_iota(jnp.int32, (rows, rows), 0)
    c_i = lax.broadcasted_iota(jnp.int32, (rows, rows), 1)
    sh = int(math.log2(c))
    same_chunk = lax.shift_right_logical(r_i, sh) == lax.shift_right_logical(c_i, sh)
    g_all = _dot01(jnp.logical_and(same_chunk, c_i <= r_i).astype(BF16), gb)
    lane = lax.broadcasted_iota(jnp.int32, gb.shape, 1)
    x = jnp.where(lane < nh, g_all, gb)
    s64 = _dot01_r(x, e64_ref[...])
    s128 = _dot01_r(x, e128_ref[...])
    eg_all = jnp.exp(s128[:, :MIX_W])
    eg_ref[...] = eg_all

    t_i = lax.broadcasted_iota(jnp.int32, (c, hc), 0)
    s_i = jnp.bitwise_and(lax.broadcasted_iota(jnp.int32, (c, hc), 1), c - 1)
    causal = s_i <= t_i
    strict = s_i < t_i
    eye = s_i == t_i
    blk_of_lane = lax.shift_right_logical(lax.broadcasted_iota(jnp.int32, (c, hc), 1), sh)
    bd_mask = bd_ref[...]

    def block_diag(x_bf):
        return jnp.concatenate([x_bf] * nh, axis=0) * bd_mask

    def diag_blocks(p):
        out = None
        for h in range(nh):
            sel = jnp.where(blk_of_lane == h, p[h * c:(h + 1) * c, :], 0.0)
            out = sel if out is None else out + sel
        return out

    def heads_on_rows(x_bf):
        return jnp.concatenate([x_bf[:, h * hw:(h + 1) * hw] for h in range(nh)], axis=0)

    tinv, pw = [], []
    for ci in range(nch):
        rs = slice(ci * c, (ci + 1) * c)
        g64, b64 = s64[rs, :hc], s64[rs, hc:]
        g_row = jnp.sum(jnp.where(eye, g64, 0.0), axis=0, keepdims=True)
        decay = jnp.exp(jnp.where(causal, g64 - g_row, -1e30))
        k_rows = heads_on_rows(k_bf[rs])
        prod = _dot_nt(jnp.concatenate([k_rows, heads_on_rows(q_bf[rs])], axis=0), k_rows)
        nmat = jnp.where(strict, b64 * decay * diag_blocks(prod[:hc]), 0.0)
        aqk_ref[rs, :] = jnp.where(causal, diag_blocks(prod[hc:]) * decay, 0.0).astype(aqk_ref.dtype)
        tinv.append(jnp.where(eye, 1.0, 0.0) - nmat)
        pw.append(nmat.astype(BF16))
    for _ in range(sh - 1):
        pw = [_dot(p, block_diag(p)).astype(BF16) for p in pw]
        tinv = [t + _dot(t.astype(BF16), block_diag(p)) for t, p in zip(tinv, pw)]
    for ci in range(nch):
        rs = slice(ci * c, (ci + 1) * c)
        g128, b128 = s128[rs, :MIX_W], s128[rs, MIX_W:]
        bk = (b128 * eg_all[rs] * k[rs]).astype(BF16)
        bv = (b128 * v[rs]).astype(BF16)
        rhs = jnp.concatenate([heads_on_rows(bk), heads_on_rows(bv)], axis=1)
        wu = _dot(block_diag(tinv[ci].astype(BF16)), rhs)
        for h in range(nh):
            cols = slice(h * hw, (h + 1) * hw)
            w_ref[rs, cols] = wu[h * c:(h + 1) * c, :hw].astype(w_ref.dtype)
            u_ref[rs, cols] = wu[h * c:(h + 1) * c, hw:]
        kd_ref[rs, :] = (k[rs] * jnp.exp(g128[c - 1:c, :] - g128)).astype(kd_ref.dtype)


def _gdn_prep(qkv, conv_buf8, conv_w, gb):
    bsz, length, _ = qkv.shape
    c, nh, hw = GDN_CHUNK, GDN_HEADS, GDN_DIM
    rows = 4 * c if length % (4 * c) == 0 else c
    per = rows // SUBLANE
    src = jnp.arange(LANE)[:, None]

    def spread(width):
        dst = jnp.arange(2 * nh * width)[None, :]
        return (src == dst // width).astype(BF16)

    e64, e128 = spread(c), spread(hw)
    blk = jnp.arange(nh * c) // c
    bd_mask = (blk[:, None] == blk[None, :]).astype(BF16)
    cur = lambda w: pl.BlockSpec((None, rows, w), lambda b, i: (b, i, 0))
    prev = pl.BlockSpec((None, SUBLANE, CONV_CH), lambda b, i: (b, jnp.maximum(i * per - 1, 0), 0))
    buf = pl.BlockSpec((None, SUBLANE, CONV_CH), lambda b, i: (b, 0, 0))
    shapes = [(MIX_W, BF16), (MIX_W, F32), (MIX_W, BF16), (MIX_W, BF16),
              (GDN_HEADS * c, BF16), (MIX_W, F32)]
    return pl.pallas_call(
        _gdn_prep_body,
        grid=(bsz, length // rows),
        in_specs=[cur(CONV_CH), prev, buf, _full(conv_w.shape), cur(LANE),
                  _full(e64.shape), _full(e128.shape), _full(bd_mask.shape)],
        out_specs=[cur(w) for w, _ in shapes],
        out_shape=[jax.ShapeDtypeStruct((bsz, length, w), d) for w, d in shapes],
        compiler_params=_cparams(("parallel", "parallel")),
        name="gdn_prep",
    )(qkv, qkv, conv_buf8, conv_w, gb, e64, e128, bd_mask)


def _gdn_seq_body(w_ref, u_ref, kd_ref, q_ref, aqk_ref, eg_ref, szd_ref, s0_ref, ng_ref,
                  out_ref, s_ref):
    c = GDN_CHUNK
    i = pl.program_id(1)

    @pl.when(i == 0)
    def _():
        s_ref[...] = s0_ref[...]

    hw = GDN_DIM
    items = [(bb, h) for bb in range(s_ref.shape[0]) for h in range(GDN_HEADS)]
    cols = lambda h: slice(h * hw, (h + 1) * hw)
    s_bf = [s_ref[bb, h].astype(BF16) for bb, h in items]
    ws_qs = [_dot(jnp.concatenate([w_ref[bb, :, cols(h)], q_ref[bb, :, cols(h)]], axis=0), sb)
             for (bb, h), sb in zip(items, s_bf)]
    d_bf = [(u_ref[bb, :, cols(h)] - x[:c]).astype(BF16) for (bb, h), x in zip(items, ws_qs)]
    upd = [_dot(jnp.concatenate([aqk_ref[bb, :, h * c:(h + 1) * c], kd_ref[bb, :, cols(h)].T], axis=0),
                d) for (bb, h), d in zip(items, d_bf)]
    for (bb, h), y in zip(items, upd):
        s_ref[bb, h] = eg_ref[bb, c - 1:c, cols(h)] * s_ref[bb, h] + y[c:]
    o = [eg_ref[bb, :, cols(h)] * x[c:] + y[:c] for (bb, h), x, y in zip(items, ws_qs, upd)]
    ms = [jnp.mean(x * x, axis=-1, keepdims=True) for x in o]
    for (bb, h), x, m in zip(items, o, ms):
        out_ref[bb, :, cols(h)] = (x * lax.rsqrt(m + EPS) * ng_ref[...]
                                   * szd_ref[bb, :, cols(h)]).astype(out_ref.dtype)


def _gdn_seq(w, u, kd, q, aqk, eg, szd, s0, norm_g):
    bsz, length, _ = w.shape
    c = GDN_CHUNK
    nbb = 4
    cur = lambda width: pl.BlockSpec((nbb, c, width), lambda b, i: (b, i, 0))
    st = pl.BlockSpec((nbb, GDN_HEADS, GDN_DIM, GDN_DIM), lambda b, i: (b, 0, 0, 0))
    return pl.pallas_call(
        _gdn_seq_body,
        grid=(bsz // nbb, length // c),
        in_specs=[cur(MIX_W), cur(MIX_W), cur(MIX_W), cur(MIX_W), cur(GDN_HEADS * c), cur(MIX_W),
                  cur(MIX_W), st, _full((1, GDN_DIM))],
        out_specs=[cur(MIX_W), st],
        out_shape=[jax.ShapeDtypeStruct((bsz, length, MIX_W), BF16),
                   jax.ShapeDtypeStruct(s0.shape, F32)],
        compiler_params=_cparams(("parallel", "arbitrary")),
        name="gdn_seq",
    )(w, u, kd, q, aqk, eg, szd, s0, norm_g)


def _row_tile(n):
    for tm in (512, 256, 128, 64):
        if n % tm == 0:
            return tm
    raise ValueError(n)


def _trunk(x, p, past, wts, s5w):
    bsz, length, _ = x.shape
    n = bsz * length
    x2d = x.reshape(n, D_MODEL)
    first = past is None
    chunk = min(length, MLP_CHUNK)
    time_major = length >= 512
    nb = bsz if time_major else 1
    tm = _row_tile(n // nb)

    outs = _even_in(x2d, wts["norm_g0"], wts["even_w_in"], wts["a_ln_g"], wts["a_ln_b"],
                    wts["a_w_s"][:, :chunk, :chunk], wts["a_b_s"][:, :chunk, :],
                    nb=nb, chunk=chunk, tm=tm, want_av=not first)
    a_out, ub, szb = outs[:3]
    a_v = None if first else outs[3].reshape(1, bsz, length, MIX_W)
    if time_major:
        ub_tm = ub.reshape(length, bsz, MIX_W)
        szb_tm = szb.reshape(length, bsz, MIX_W)
    else:
        ub_tm = ub.reshape(bsz, length, MIX_W).transpose(1, 0, 2)
        szb_tm = szb.reshape(bsz, length, MIX_W).transpose(1, 0, 2)
    if first:
        x0r = jnp.zeros((bsz, S5_LANES), F32)
        x0i = x0r
    else:
        x0r = past["b_re"].reshape(bsz, S5_LANES)
        x0i = past["b_im"].reshape(bsz, S5_LANES)
    b_out_tm, xr, xi = _s5(ub_tm, szb_tm, x0r, x0i, *s5w, wts["b_D"], wts["b_glu_w"],
                           t_steps=min(length, 64))
    if time_major:
        b_out = b_out_tm.reshape(length, bsz * MIX_W)
    else:
        b_out = b_out_tm.transpose(1, 0, 2).reshape(n, MIX_W)
    p3d = p.reshape(p.shape[0], n, PLE_DIM)
    h = _mix_out(a_out, b_out, x2d, p3d, 0, wts["even_w_out"], wts["ple_norm_g0"], wts["ple_gate_w0"],
                 wts["ple_proj0"], None, nb=nb, tm=tm, b_time_major=time_major)

    tm1 = _row_tile(n)
    q, k, v, szc, qkv, szd, gb = _odd_in(h, wts["norm_g1"], wts["odd_w_main"], wts["odd_w_ab"],
                                         wts["d_A_log"], wts["d_dt_bias"], tm=min(tm1, 256))
    to3 = lambda a: a.reshape(bsz, length, a.shape[-1])
    q, k, v, szc, qkv, szd, gb = map(to3, (q, k, v, szc, qkv, szd, gb))
    if first:
        c_out = _sb_attention(q, k, v, szc)
        buf8 = jnp.zeros((bsz, SUBLANE, CONV_CH), F32)
        s0 = jnp.zeros((bsz, GDN_HEADS, GDN_DIM, GDN_DIM), F32)
    else:
        c_out = _sb_attention(q, k, v, szc, past["k_c"], past["v_c"])
        buf8 = jnp.pad(past["conv_d"], ((0, 0), (SUBLANE - (CONV_W - 1), 0), (0, 0)))
        s0 = past["s_d"]
    w_, u_, kd_, qd_, aqk_, eg_ = _gdn_prep(qkv, buf8, wts["d_conv_w"], gb)
    d_out, s_new = _gdn_seq(w_, u_, kd_, qd_, aqk_, eg_, szd, s0, wts["d_norm_g"])
    y = _mix_out(c_out.reshape(n, MIX_W), d_out.reshape(n, MIX_W), h, p3d, 1, wts["odd_w_out"],
                 wts["ple_norm_g1"], wts["ple_gate_w1"], wts["ple_proj1"], wts["final_norm_g"],
                 nb=1, tm=tm1, b_time_major=False)

    conv_new = qkv[:, length - (CONV_W - 1):, :]
    return (y.reshape(bsz, length, D_MODEL),
            xr.reshape(1, bsz, S5_GROUPS, S5_N), xi.reshape(1, bsz, S5_GROUPS, S5_N), a_v,
            k.reshape(1, bsz, length, SB_HEADS, SB_DIM), v.reshape(1, bsz, length, SB_HEADS, SB_DIM),
            s_new[None], conv_new[None])


def kernel(x_prompt, x_sample, state_b_re, state_b_im, cache_k_c, cache_v_c, state_d, state_conv_d,
           p_prompt, p_sample,
           norm_g, final_norm_g, ple_proj, ple_gate_w, ple_norm_g,
           even_w_in, even_w_out, a_ln_g, a_ln_b, a_w_s, a_b_s,
           b_lam_re, b_lam_im, b_log_dt, b_B_re, b_B_im, b_C_re, b_C_im, b_D, b_glu_w,
           odd_w_in, odd_w_out, d_conv_w, d_A_log, d_dt_bias, d_norm_g):
    row = lambda a: a.reshape(1, -1).astype(F32)
    tril = jnp.tril(jnp.ones((MLP_CHUNK, MLP_CHUNK), dtype=bool))
    n_main = 8 * MIX_W
    pad_lane = lambda a: jnp.pad(a.reshape(1, -1).astype(F32), ((0, 0), (0, LANE - a.size)))
    wts = {
        "norm_g0": row(norm_g[0]), "norm_g1": row(norm_g[1]), "final_norm_g": row(final_norm_g),
        "ple_norm_g0": row(ple_norm_g[0]), "ple_norm_g1": row(ple_norm_g[1]),
        "ple_gate_w0": ple_gate_w[0].astype(BF16), "ple_gate_w1": ple_gate_w[1].astype(BF16),
        "ple_proj0": ple_proj[0].astype(BF16), "ple_proj1": ple_proj[1].astype(BF16),
        "even_w_in": even_w_in[0].astype(BF16), "even_w_out": even_w_out[0].astype(BF16),
        "a_ln_g": row(a_ln_g[0]), "a_ln_b": row(a_ln_b[0]),
        "a_w_s": jnp.where(tril, a_w_s[0], 0.0).astype(BF16),
        "a_b_s": jnp.broadcast_to(a_b_s[0][:, :, None], (A_GROUPS, MLP_CHUNK, LANE)).astype(F32),
        "b_D": row(b_D[0]), "b_glu_w": b_glu_w[0].astype(BF16),
        "odd_w_main": odd_w_in[0][:, :n_main].astype(BF16),
        "odd_w_ab": jnp.pad(odd_w_in[0][:, n_main:], ((0, 0), (0, LANE - 2 * GDN_HEADS))).astype(BF16),
        "odd_w_out": odd_w_out[0].astype(BF16),
        "d_conv_w": d_conv_w[0].astype(F32),
        "d_A_log": pad_lane(d_A_log[0]), "d_dt_bias": pad_lane(d_dt_bias[0]),
        "d_norm_g": row(d_norm_g[0]),
    }
    ar, ai, bbr, bbi = _s5_disc(b_lam_re[0], b_lam_im[0], b_log_dt[0], b_B_re[0], b_B_im[0])
    bre, bim, cre, cim = _s5_block_diag((bbr, bbi), b_C_re[0], b_C_im[0])
    s5w = (ar.reshape(1, S5_LANES), ai.reshape(1, S5_LANES), bre, bim, cre, cim)

    (y_p, b_re_p, b_im_p, _, k_p, v_p, s_p, conv_p) = _trunk(x_prompt, p_prompt, None, wts, s5w)
    past = {"b_re": state_b_re[0], "b_im": state_b_im[0],
            "k_c": cache_k_c[0], "v_c": cache_v_c[0],
            "s_d": state_d[0], "conv_d": state_conv_d[0]}
    (y_s, b_re_s, b_im_s, a_v_s, k_s, v_s, s_s, conv_s) = _trunk(x_sample, p_sample, past, wts, s5w)
    return (y_p, y_s, b_re_p, b_im_p, k_p, v_p, s_p, conv_p,
            b_re_s, b_im_s, a_v_s, k_s, v_s, s_s, conv_s)
```

```python
import functools
import math

import jax
import jax.numpy as jnp
from jax import lax
from jax.experimental import pallas as pl
from jax.experimental.pallas import tpu as pltpu

F32 = jnp.float32
BF16 = jnp.bfloat16
EPS = 1e-6

D_MODEL = 1024
PLE_DIM = 256
MIX_W = 512
A_GROUPS = 4
MLP_CHUNK = 128
S5_GROUPS = 32
S5_P = 16
S5_N = 64
S5_LANES = S5_GROUPS * S5_N
SB_HEADS = 8
SB_DIM = 64
SB_BLOCK = 128
GDN_HEADS = 4
GDN_DIM = 128
GDN_CHUNK = 64
CONV_W = 4
CONV_CH = 3 * MIX_W
LANE = 128
SUBLANE = 8
VMEM_LIMIT = 52 * 1024 * 1024
SB_DEAD = -104.0


def _cparams(sem):
    return pltpu.CompilerParams(dimension_semantics=sem, vmem_limit_bytes=VMEM_LIMIT)


def _gelu(x):
    return 0.5 * x * (1.0 + jnp.tanh(0.7978845608028654 * (x + 0.044715 * (x * x * x))))


def _sigmoid(x):
    return 1.0 / (1.0 + jnp.exp(-x))


def _silu(x):
    return x * _sigmoid(x)


def _softplus(x):
    return jnp.maximum(x, 0.0) + jnp.log1p(jnp.exp(-jnp.abs(x)))


def _rms(x, g):
    ms = jnp.mean(x * x, axis=-1, keepdims=True)
    return x * lax.rsqrt(ms + EPS) * g


def _dot(a, b):
    return jnp.dot(a, b, preferred_element_type=F32)


def _dot_nt(a, b):
    return lax.dot_general(a, b, (((1,), (1,)), ((), ())), preferred_element_type=F32)


def _dot_tn(a, b):
    return lax.dot_general(a, b, (((0,), (0,)), ((), ())), preferred_element_type=F32)


def _split3(x):
    hi = x.astype(BF16)
    r = x - hi.astype(F32)
    mid = r.astype(BF16)
    lo = (r - mid.astype(F32)).astype(BF16)
    return hi, mid, lo


def _full(shape):
    n = len(shape)
    return pl.BlockSpec(shape, lambda *_: (0,) * n)


def _time_major(x, nseq):
    tm = x.shape[0] // nseq
    return pltpu.einshape("btd->tbd", x.reshape(nseq, tm, x.shape[1]))


def _even_in_body(x_ref, g_ref, w_ref, lng_ref, lnb_ref, ws_ref, bs_ref,
                  aout_ref, ub_ref, szb_ref, av_ref, *, chunk):
    nseq, tm, _ = x_ref.shape
    rows_all = nseq * tm
    hn = _rms(x_ref[...].reshape(rows_all, D_MODEL), g_ref[...]).astype(BF16)

    def proj(k):
        return _dot(hn, w_ref[:, k * MIX_W:(k + 1) * MIX_W])

    va = _gelu(proj(1))
    mu = jnp.mean(va, axis=-1, keepdims=True)
    vc = va - mu
    var = jnp.mean(vc * vc, axis=-1, keepdims=True)
    va = vc * lax.rsqrt(var + EPS) * lng_ref[...] + lnb_ref[...]
    if av_ref is not None:
        av_ref[...] = va.reshape(av_ref.shape)
    va_bf = va.astype(BF16)
    gate = _gelu(proj(0)) * _silu(proj(2))
    gw = MIX_W // A_GROUPS
    pieces = []
    for ch in range(rows_all // chunk):
        rows = slice(ch * chunk, (ch + 1) * chunk)
        s = [_dot(ws_ref[g], va_bf[rows, g * gw:(g + 1) * gw]) + bs_ref[g] for g in range(A_GROUPS)]
        pieces.append(gate[rows] * jnp.concatenate(s, axis=1))
    aout_ref[...] = jnp.concatenate(pieces, axis=0).astype(aout_ref.dtype).reshape(aout_ref.shape)
    ub_ref[...] = _time_major(proj(3), nseq)
    szb_ref[...] = _time_major(_silu(proj(4)), nseq)


def _even_in(x, norm_g, w_in, ln_g, ln_b, w_s, b_s, *, chunk, tm, want_av):
    bsz, length, _ = x.shape
    body = functools.partial(_even_in_body, chunk=chunk)
    if not want_av:
        body_fn = lambda *r: body(*r, None)
    else:
        body_fn = body
    seq = lambda w: pl.BlockSpec((SUBLANE, tm, w), lambda b, i: (b, i, 0))
    tmaj = pl.BlockSpec((tm, SUBLANE, MIX_W), lambda b, i: (i, b, 0))
    out_shape = [jax.ShapeDtypeStruct((bsz, length, MIX_W), BF16),
                 jax.ShapeDtypeStruct((length, bsz, MIX_W), F32),
                 jax.ShapeDtypeStruct((length, bsz, MIX_W), F32)]
    out_specs = [seq(MIX_W), tmaj, tmaj]
    if want_av:
        out_shape.append(jax.ShapeDtypeStruct((bsz, length, MIX_W), F32))
        out_specs.append(seq(MIX_W))
    return pl.pallas_call(
        body_fn,
        grid=(bsz // SUBLANE, length // tm),
        in_specs=[seq(D_MODEL),
                  _full((1, D_MODEL)),
                  _full(w_in.shape),
                  _full((1, MIX_W)), _full((1, MIX_W)),
                  _full(w_s.shape), _full(b_s.shape)],
        out_specs=out_specs,
        out_shape=out_shape,
        compiler_params=_cparams(("parallel", "parallel")),
        name="even_in",
    )(x, norm_g, w_in, ln_g, ln_b, w_s, b_s)


def _s5_disc_body(lr_ref, li_ref, ldt_ref, br_ref, bi_ref, ar_ref, ai_ref, bbr_ref, bbi_ref):
    dt = jnp.exp(ldt_ref[...])
    lr, li = lr_ref[...], li_ref[...]
    mag = jnp.exp(lr * dt)
    ar = mag * jnp.cos(li * dt)
    ai = mag * jnp.sin(li * dt)
    den = lr * lr + li * li
    cr = ((ar - 1.0) * lr + ai * li) / den
    ci = (ai * lr - (ar - 1.0) * li) / den
    ar_ref[...] = ar
    ai_ref[...] = ai
    for p in range(S5_P):
        br, bi = br_ref[p], bi_ref[p]
        bbr_ref[p] = cr * br - ci * bi
        bbi_ref[p] = cr * bi + ci * br


def _s5_disc(lam_re, lam_im, log_dt, b_re, b_im):
    gn = (S5_GROUPS, S5_N)
    pgn = (S5_P, S5_GROUPS, S5_N)
    return pl.pallas_call(
        _s5_disc_body,
        out_shape=[jax.ShapeDtypeStruct(gn, F32), jax.ShapeDtypeStruct(gn, F32),
                   jax.ShapeDtypeStruct(pgn, F32), jax.ShapeDtypeStruct(pgn, F32)],
        name="s5_disc",
    )(lam_re, lam_im, log_dt.reshape(S5_GROUPS, 1),
      jnp.transpose(b_re, (2, 0, 1)), jnp.transpose(b_im, (2, 0, 1)))


def _s5_block_diag(bb_pgn, c_re, c_im):
    gpc = LANE // S5_P
    nj = S5_GROUPS // gpc
    eye = jnp.eye(gpc, dtype=F32)

    def b_blk(bb):
        b = jnp.transpose(bb, (1, 0, 2)).reshape(nj, gpc, S5_P, S5_N)
        m = b[:, :, :, None, :] * eye[None, :, None, :, None]
        return m.reshape(nj, gpc * S5_P, gpc * S5_N).astype(BF16)

    def c_blk(c):
        cc = jnp.transpose(c, (0, 2, 1)).reshape(nj, gpc, S5_N, S5_P)
        m = cc[:, :, :, None, :] * eye[None, :, None, :, None]
        return m.reshape(nj, gpc * S5_N, gpc * S5_P).astype(BF16)

    return b_blk(bb_pgn[0]), b_blk(bb_pgn[1]), c_blk(c_re), c_blk(-c_im)


def _s5_body(u_ref, szb_ref, x0r_ref, x0i_ref, ar_ref, ai_ref, bre_ref, bim_ref, cre_ref, cim_ref,
             d_ref, glu_ref, out_ref, st_r, st_i, sre, sim):
    t_steps = u_ref.shape[0]
    rows = t_steps * SUBLANE
    ti = pl.program_id(1)
    cw = bre_ref.shape[2]
    nj = bre_ref.shape[0]

    @pl.when(ti == 0)
    def _():
        st_r[...] = x0r_ref[...]
        st_i[...] = x0i_ref[...]

    u = u_ref[...].reshape(rows, MIX_W)
    u_bf = u.astype(BF16)
    for j in range(nj):
        uj = u_bf[:, j * LANE:(j + 1) * LANE]
        sre[:, j * cw:(j + 1) * cw] = _dot(uj, bre_ref[j])
        sim[:, j * cw:(j + 1) * cw] = _dot(uj, bim_ref[j])

    for j in range(nj):
        lanes = slice(j * cw, (j + 1) * cw)
        a_r = jnp.broadcast_to(ar_ref[:, lanes], (SUBLANE, cw))
        a_i = jnp.broadcast_to(ai_ref[:, lanes], (SUBLANE, cw))

        def step(t, carry, lanes=lanes, a_r=a_r, a_i=a_i):
            xr, xi = carry
            r0 = pl.multiple_of(t * SUBLANE, SUBLANE)
            nxr = a_r * xr - a_i * xi + sre[pl.ds(r0, SUBLANE), lanes]
            nxi = a_r * xi + a_i * xr + sim[pl.ds(r0, SUBLANE), lanes]
            sre[pl.ds(r0, SUBLANE), lanes] = nxr
            sim[pl.ds(r0, SUBLANE), lanes] = nxi
            return nxr, nxi

        xr, xi = lax.fori_loop(0, t_steps, step, (st_r[:, lanes], st_i[:, lanes]), unroll=8)
        st_r[:, lanes] = xr
        st_i[:, lanes] = xi

    ys = []
    for j in range(nj):
        lanes = slice(j * cw, (j + 1) * cw)
        ys.append(_dot(sre[:, lanes].astype(BF16), cre_ref[j])
                  + _dot(sim[:, lanes].astype(BF16), cim_ref[j]))
    y = jnp.concatenate(ys, axis=1) + d_ref[...] * u
    y = _gelu(y)
    y = y * _sigmoid(_dot(y.astype(BF16), glu_ref[...]))
    y = y * szb_ref[...].reshape(rows, MIX_W)
    out_ref[...] = y.reshape(out_ref.shape)


def _s5(u_tm, szb_tm, x0r, x0i, ar, ai, bre, bim, cre, cim, d_skip, glu_w, *, t_steps):
    length, nb, _ = u_tm.shape
    blk = pl.BlockSpec((t_steps, SUBLANE, MIX_W), lambda b, t: (t, b, 0))
    st = pl.BlockSpec((SUBLANE, S5_LANES), lambda b, t: (b, 0))
    rows = t_steps * SUBLANE
    return pl.pallas_call(
        _s5_body,
        grid=(nb // SUBLANE, length // t_steps),
        in_specs=[blk, blk, st, st, _full((1, S5_LANES)), _full((1, S5_LANES)),
                  _full(bre.shape), _full(bim.shape), _full(cre.shape), _full(cim.shape),
                  _full((1, MIX_W)), _full(glu_w.shape)],
        out_specs=[blk, st, st],
        out_shape=[jax.ShapeDtypeStruct((length, nb, MIX_W), F32),
                   jax.ShapeDtypeStruct((nb, S5_LANES), F32),
                   jax.ShapeDtypeStruct((nb, S5_LANES), F32)],
        scratch_shapes=[pltpu.VMEM((rows, S5_LANES), F32), pltpu.VMEM((rows, S5_LANES), F32)],
        compiler_params=_cparams(("parallel", "arbitrary")),
        name="s5_scan",
    )(u_tm, szb_tm, x0r, x0i, ar, ai, bre, bim, cre, cim, d_skip, glu_w)


def _mix_out_body(a_ref, b_ref, h_ref, p_ref, wout_ref, pg_ref, wgate_ref, wp_ref, fg_ref, out_ref,
                  *, b_time_major):
    rows = out_ref.shape[0] * out_ref.shape[1]
    flat = lambda ref: ref[...].reshape(rows, ref.shape[-1])
    if b_time_major:
        b = jnp.concatenate([b_ref[:, s, :] for s in range(b_ref.shape[1])], axis=0)
    else:
        b = flat(b_ref)
    mix = _dot(flat(a_ref), wout_ref[:MIX_W, :]) + _dot(b.astype(BF16), wout_ref[MIX_W:, :])
    h1 = flat(h_ref) + mix
    gate = _sigmoid(_dot(_rms(h1, pg_ref[...]).astype(BF16), wgate_ref[...]))
    h2 = h1 + gate * _dot(flat(p_ref).astype(BF16), wp_ref[...])
    if fg_ref is not None:
        h2 = _rms(h2, fg_ref[...])
    out_ref[...] = h2.reshape(out_ref.shape)


def _mix_out(a, b, h, p, layer, w_out, ple_g, w_gate, w_p, final_g, *, tm, b_time_major):
    bsz, length, _ = h.shape
    seq = lambda w: pl.BlockSpec((SUBLANE, tm, w), lambda s, i: (s, i, 0))
    b_spec = pl.BlockSpec((tm, SUBLANE, MIX_W), lambda s, i: (i, s, 0)) if b_time_major else seq(MIX_W)
    args = [a, b, h, p, w_out, ple_g, w_gate, w_p]
    in_specs = [seq(MIX_W), b_spec, seq(D_MODEL),
                pl.BlockSpec((None, SUBLANE, tm, PLE_DIM), lambda s, i: (layer, s, i, 0)),
                _full(w_out.shape), _full((1, D_MODEL)), _full(w_gate.shape), _full(w_p.shape)]
    body = functools.partial(_mix_out_body, b_time_major=b_time_major)
    if final_g is None:
        body_fn = lambda *r: body(*r[:8], None, r[8])
    else:
        body_fn = body
        args.append(final_g)
        in_specs.append(_full((1, D_MODEL)))
    return pl.pallas_call(
        body_fn,
        grid=(bsz // SUBLANE, length // tm),
        in_specs=in_specs,
        out_specs=seq(D_MODEL),
        out_shape=jax.ShapeDtypeStruct((bsz, length, D_MODEL), F32),
        compiler_params=_cparams(("parallel", "parallel")),
        name="mix_out",
    )(*args)


def _odd_in_body(h_ref, g_ref, w_ref, wab_ref, alog_ref, dtb_ref,
                 q_ref, k_ref, v_ref, szc_ref, qkv_ref, szd_ref, gb_ref, k4_ref, v4_ref):
    hn = _rms(h_ref[...], g_ref[...]).astype(BF16)

    def proj(k, width=MIX_W):
        return _dot(hn, w_ref[:, k * MIX_W:k * MIX_W + width])

    def head_major(x):
        return pltpu.einshape("t(hd)->thd", x, h=SB_HEADS)

    q_ref[...] = (proj(0) * (SB_DIM ** -0.5)).astype(BF16)
    kc, vc = proj(1), proj(2)
    k_ref[...] = kc.astype(BF16)
    v_ref[...] = vc.astype(BF16)
    k4_ref[...] = head_major(kc)
    v4_ref[...] = head_major(vc)
    szc_ref[...] = _silu(proj(3))
    for k in range(3):
        qkv_ref[:, k * MIX_W:(k + 1) * MIX_W] = proj(4 + k)
    szd_ref[...] = _silu(proj(7))
    ab = _dot(hn, wab_ref[...])
    g = -jnp.exp(alog_ref[...]) * _softplus(ab + dtb_ref[...])
    lane = lax.broadcasted_iota(jnp.int32, ab.shape, 1)
    gb_ref[...] = jnp.where(lane < GDN_HEADS, g, _sigmoid(ab))


def _odd_in(h2d, norm_g, w_main, w_ab, a_log, dt_bias, *, tm):
    n = h2d.shape[0]
    row = lambda i: (i, 0)
    widths = [MIX_W, MIX_W, MIX_W, MIX_W, CONV_CH, MIX_W, LANE]
    dtypes = [BF16, BF16, BF16, F32, F32, F32, F32]
    head4 = pl.BlockSpec((tm, SB_HEADS, SB_DIM), lambda i: (i, 0, 0))
    head4_shape = jax.ShapeDtypeStruct((n, SB_HEADS, SB_DIM), F32)
    return pl.pallas_call(
        _odd_in_body,
        grid=(n // tm,),
        in_specs=[pl.BlockSpec((tm, D_MODEL), row), _full((1, D_MODEL)),
                  _full(w_main.shape), _full(w_ab.shape), _full((1, LANE)), _full((1, LANE))],
        out_specs=[pl.BlockSpec((tm, w), row) for w in widths] + [head4, head4],
        out_shape=[jax.ShapeDtypeStruct((n, w), d) for w, d in zip(widths, dtypes)]
        + [head4_shape, head4_shape],
        compiler_params=_cparams(("parallel",)),
        name="odd_in",
    )(h2d, norm_g, w_main, w_ab, a_log, dt_bias)


def _sb_pairs(q2x, k_blocks, v_blocks, masks, valids, c_in, cum_rhs, key_major):
    npair = len(q2x)
    nblk = len(k_blocks[0])
    rows = q2x[0].shape[0]
    if key_major:
        z = [_dot(q2x[p], jnp.concatenate(k_blocks[p], axis=1)) for p in range(npair)]
    else:
        z = [_dot_nt(q2x[p], jnp.concatenate(k_blocks[p], axis=0)) for p in range(npair)]
    log_beta, hilo = [], []
    for p in range(npair):
        lb, hl = [], []
        for j in range(nblk):
            zj = z[p][:, j * LANE:(j + 1) * LANE]
            l1p = jnp.log(1.0 + jnp.exp(-jnp.abs(zj)))
            lb.append(jnp.minimum(zj, 0.0) - l1p)
            log_keep = -jnp.maximum(zj, 0.0) - l1p
            if masks[j] is not None:
                log_keep = jnp.where(masks[j], log_keep, 0.0)
            if valids[j] is not None:
                log_keep = jnp.where(valids[j], log_keep, 0.0)
            hi = log_keep.astype(BF16)
            lo = (log_keep - hi.astype(F32)).astype(BF16)
            hl.append(jnp.concatenate([hi, lo], axis=1))
        log_beta.append(lb)
        hilo.append(jnp.concatenate(hl, axis=0))
    cum = [_dot(x, cum_rhs) for x in hilo]
    ws, totals = [], []
    for p in range(npair):
        c = c_in[p]
        w_p = [None] * nblk
        for j in reversed(range(nblk)):
            after = cum[p][j * rows:(j + 1) * rows, :LANE]
            if c is not None:
                after = after + c
            w = jnp.exp(log_beta[p][j] + after)
            if masks[j] is not None:
                w = jnp.where(masks[j], w, 0.0)
            if valids[j] is not None:
                w = jnp.where(valids[j], w, 0.0)
            w_p[j] = w.astype(BF16)
            total = cum[p][j * rows:(j + 1) * rows, LANE:]
            c = total if c is None else c + total
        ws.append(jnp.concatenate(w_p, axis=1))
        totals.append(c)
    if key_major:
        pv = [_dot_nt(ws[p], jnp.concatenate(v_blocks[p], axis=1)) for p in range(npair)]
    else:
        pv = [_dot(ws[p], jnp.concatenate(v_blocks[p], axis=0)) for p in range(npair)]
    return pv, totals


def _sb_body(q_ref, kd_ref, vd_ref, k1_ref, v1_ref, k2_ref, v2_ref, szc_ref, cum_ref,
             kold_ref, vold_ref, out_ref, acc_ref, c_ref, kbuf, vbuf, sem, *, top_of, key_major):
    b = pl.program_id(0)
    i = pl.program_id(1)
    tq = q_ref.shape[0]
    tk = kbuf.shape[-1] if key_major else kbuf.shape[0]
    npair = MIX_W // LANE
    pairs = range(npair)
    tt = lax.broadcasted_iota(jnp.int32, (2 * tq, tk), 0)
    ss = lax.broadcasted_iota(jnp.int32, (2 * tq, tk), 1)
    diag_mask = ss < jnp.where(tt >= tq, tt - tq, tt)
    even_lane = lax.broadcasted_iota(jnp.int32, (tq, LANE), 1) < SB_DIM
    top = top_of(i)

    def diag_tile(ref, p):
        x = ref[:, p * LANE:(p + 1) * LANE]
        if key_major:
            x = x.astype(F32).T
            if tq < tk:
                x = jnp.concatenate([x, jnp.zeros((LANE, tk - tq), F32)], axis=1)
        elif tq < tk:
            x = jnp.concatenate([x, jnp.zeros((tk - tq, LANE), x.dtype)], axis=0)
        return x.astype(BF16)

    def old_tile(ref, p):
        if key_major:
            return jnp.concatenate([ref[2 * p], ref[2 * p + 1]], axis=0).astype(BF16)
        return ref[:, p * LANE:(p + 1) * LANE].astype(BF16)

    def q_pair(p):
        q2 = q_ref[:, p * LANE:(p + 1) * LANE]
        zero = jnp.zeros_like(q2)
        return jnp.concatenate([jnp.where(even_lane, q2, zero), jnp.where(even_lane, zero, q2)], axis=0)

    def merge(pv):
        return jnp.where(even_lane, pv[:tq], pv[tq:])

    q2x = [q_pair(p) for p in pairs]
    k_win = [[old_tile(k2_ref, p), old_tile(k1_ref, p), diag_tile(kd_ref, p)] for p in pairs]
    v_win = [[old_tile(v2_ref, p), old_tile(v1_ref, p), diag_tile(vd_ref, p)] for p in pairs]
    masks = [None, None, diag_mask]
    valids = [top >= 1, top >= 0, None]
    cum_rhs = cum_ref[...]
    pvs, sums = _sb_pairs(q2x, k_win, v_win, masks, valids, [None] * npair, cum_rhs, key_major)
    acc_all = jnp.concatenate([merge(pv) for pv in pvs], axis=1)
    out_ref[...] = (acc_all * szc_ref[...]).astype(out_ref.dtype)
    c_max = functools.reduce(jnp.maximum, sums)

    @pl.when(jnp.logical_and(top >= 2, jnp.max(c_max) > SB_DEAD))
    def _():
        acc_ref[...] = acc_all
        for p in pairs:
            c_ref[p] = sums[p]

        def cond(carry):
            j, go = carry
            return jnp.logical_and(j >= 0, go)

        def body(carry):
            j, _ = carry
            r0 = pl.multiple_of(j * tk, tk)
            if key_major:
                src = lambda ref: ref.at[b, :, :, pl.ds(r0, tk)]
            else:
                src = lambda ref: ref.at[b, pl.ds(r0, tk)]
            ck = pltpu.make_async_copy(src(kold_ref), kbuf, sem.at[0])
            cv = pltpu.make_async_copy(src(vold_ref), vbuf, sem.at[1])
            ck.start()
            cv.start()
            ck.wait()
            cv.wait()
            pvs, cs = _sb_pairs(q2x, [[old_tile(kbuf, p)] for p in pairs],
                                [[old_tile(vbuf, p)] for p in pairs], [None], [None],
                                [c_ref[p] for p in pairs], cum_rhs, key_major)
            for p in pairs:
                acc_ref[:, p * LANE:(p + 1) * LANE] += merge(pvs[p])
                c_ref[p] = cs[p]
            return j - 1, jnp.max(c_ref[...]) > SB_DEAD

        lax.while_loop(cond, body, (top - 2, jnp.max(c_max) > SB_DEAD))
        out_ref[...] = (acc_ref[...] * szc_ref[...]).astype(out_ref.dtype)


def _sb_attention(q, k_new, v_new, szc, k_past=None, v_past=None):
    bsz, length, _ = q.shape
    tq = min(length, SB_BLOCK)
    nq = length // tq
    tk = SB_BLOCK
    key_major = k_past is not None
    if not key_major:
        assert tq == tk
        k_old, v_old = k_new, v_new
        top_of = lambda i: i - 1
        old_blk = (tk, MIX_W)
        old_idx = lambda b, blk: (b, blk, 0)
    else:
        assert nq == 1 and k_past.shape[-1] % tk == 0
        k_old, v_old = k_past, v_past
        n_old = k_past.shape[-1] // tk
        top_of = lambda i: i * 0 + (n_old - 1)
        old_blk = (SB_HEADS, SB_DIM, tk)
        old_idx = lambda b, blk: (b, 0, 0, blk)
    cur = pl.BlockSpec((None, tq, MIX_W), lambda b, i: (b, i, 0))
    old1 = pl.BlockSpec((None,) + old_blk, lambda b, i: old_idx(b, jnp.maximum(top_of(i), 0)))
    old2 = pl.BlockSpec((None,) + old_blk, lambda b, i: old_idx(b, jnp.maximum(top_of(i) - 1, 0)))
    anyspec = pl.BlockSpec(memory_space=pl.ANY)
    jj = lax.broadcasted_iota(jnp.int32, (tk, tk), 0)
    ss = lax.broadcasted_iota(jnp.int32, (tk, tk), 1)
    half = jnp.concatenate([(jj > ss).astype(BF16), jnp.ones((tk, LANE), BF16)], axis=1)
    cum_rhs = jnp.concatenate([half, half], axis=0)
    return pl.pallas_call(
        functools.partial(_sb_body, top_of=top_of, key_major=key_major),
        grid=(bsz, nq),
        in_specs=[cur, cur, cur, old1, old1, old2, old2, cur, _full(cum_rhs.shape),
                  anyspec, anyspec],
        out_specs=cur,
        out_shape=jax.ShapeDtypeStruct((bsz, length, MIX_W), BF16),
        scratch_shapes=[pltpu.VMEM((tq, MIX_W), F32),
                        pltpu.VMEM((MIX_W // LANE, 2 * tq, LANE), F32),
                        pltpu.VMEM(old_blk, k_old.dtype), pltpu.VMEM(old_blk, v_old.dtype),
                        pltpu.SemaphoreType.DMA((2,))],
        compiler_params=_cparams(("parallel", "parallel")),
        name="sb_attention",
    )(q, k_new, v_new, k_old, v_old, k_old, v_old, szc, cum_rhs, k_old, v_old)


def _dot01(m01, x):
    hi, mid, lo = _split3(x)
    return _dot(m01, hi) + _dot(m01, mid) + _dot(m01, lo)


def _dot01_r(x, m01):
    hi, mid, lo = _split3(x)
    return _dot(hi, m01) + _dot(mid, m01) + _dot(lo, m01)


def _gdn_prep_body(cur_ref, prev_ref, buf_ref, cw_ref, gb_ref, e64_ref, e128_ref, bd_ref,
                   w_ref, u_ref, kd_ref, q_ref, aqk_ref, eg_ref):
    c = GDN_CHUNK
    rows = cur_ref.shape[0]
    i = pl.program_id(1)
    cur = cur_ref[...]
    prev = jnp.where(i == 0, buf_ref[...], prev_ref[...])
    xp = jnp.concatenate([prev, cur], axis=0)
    off = SUBLANE - (CONV_W - 1)
    y = xp[off:off + rows] * cw_ref[0:1, :]
    for t in range(1, CONV_W):
        y = y + xp[off + t:off + t + rows] * cw_ref[t:t + 1, :]
    y = _silu(y)

    nh, hw = GDN_HEADS, GDN_DIM
    hc = nh * c
    nch = rows // c

    def l2n(x, scale):
        parts = []
        for h in range(nh):
            xh = x[:, h * hw:(h + 1) * hw]
            parts.append(xh * (lax.rsqrt(jnp.sum(xh * xh, axis=-1, keepdims=True) + EPS) * scale))
        return jnp.concatenate(parts, axis=1)

    q = l2n(y[:, :MIX_W], hw ** -0.5)
    k = l2n(y[:, MIX_W:2 * MIX_W], 1.0)
    v = y[:, 2 * MIX_W:]
    q_bf, k_bf = q.astype(BF16), k.astype(BF16)
    q_ref[...] = q_bf

    gb = gb_ref[...]
    r_i = lax.broadcasted_iota(jnp.int32, (rows, rows), 0)
    c_i = lax.broadcasted_iota(jnp.int32, (rows, rows), 1)
    sh = int(math.log2(c))
    same_chunk = lax.shift_right_logical(r_i, sh) == lax.shift_right_logical(c_i, sh)
    g_all = _dot01(jnp.logical_and(same_chunk, c_i <= r_i).astype(BF16), gb)
    lane = lax.broadcasted_iota(jnp.int32, gb.shape, 1)
    x = jnp.where(lane < nh, g_all, gb)
    s64 = _dot01_r(x, e64_ref[...])
    s128 = _dot01_r(x, e128_ref[...])
    eg_all = jnp.exp(s128[:, :MIX_W])
    eg_ref[...] = eg_all

    t_i = lax.broadcasted_iota(jnp.int32, (c, hc), 0)
    s_i = jnp.bitwise_and(lax.broadcasted_iota(jnp.int32, (c, hc), 1), c - 1)
    causal = s_i <= t_i
    strict = s_i < t_i
    eye = s_i == t_i
    blk_of_lane = lax.shift_right_logical(lax.broadcasted_iota(jnp.int32, (c, hc), 1), sh)
    bd_mask = bd_ref[...]

    def block_diag(x_bf):
        return jnp.concatenate([x_bf] * nh, axis=0) * bd_mask

    def diag_blocks(p):
        out = None
        for h in range(nh):
            sel = jnp.where(blk_of_lane == h, p[h * c:(h + 1) * c, :], 0.0)
            out = sel if out is None else out + sel
        return out

    def heads_on_rows(x_bf):
        return jnp.concatenate([x_bf[:, h * hw:(h + 1) * hw] for h in range(nh)], axis=0)

    tinv, pw = [], []
    for ci in range(nch):
        rs = slice(ci * c, (ci + 1) * c)
        g64, b64 = s64[rs, :hc], s64[rs, hc:]
        g_row = jnp.sum(jnp.where(eye, g64, 0.0), axis=0, keepdims=True)
        decay = jnp.exp(jnp.where(causal, g64 - g_row, -1e30))
        k_rows = heads_on_rows(k_bf[rs])
        prod = _dot_nt(jnp.concatenate([k_rows, heads_on_rows(q_bf[rs])], axis=0), k_rows)
        nmat = jnp.where(strict, b64 * decay * diag_blocks(prod[:hc]), 0.0)
        aqk_ref[rs, :] = jnp.where(causal, diag_blocks(prod[hc:]) * decay, 0.0).astype(aqk_ref.dtype)
        tinv.append(jnp.where(eye, 1.0, 0.0) - nmat)
        pw.append(nmat.astype(BF16))
    for _ in range(sh - 1):
        pw = [_dot(p, block_diag(p)).astype(BF16) for p in pw]
        tinv = [t + _dot(t.astype(BF16), block_diag(p)) for t, p in zip(tinv, pw)]
    for ci in range(nch):
        rs = slice(ci * c, (ci + 1) * c)
        g128, b128 = s128[rs, :MIX_W], s128[rs, MIX_W:]
        bk = (b128 * eg_all[rs] * k[rs]).astype(BF16)
        bv = (b128 * v[rs]).astype(BF16)
        rhs = jnp.concatenate([heads_on_rows(bk), heads_on_rows(bv)], axis=1)
        wu = _dot(block_diag(tinv[ci].astype(BF16)), rhs)
        for h in range(nh):
            cols = slice(h * hw, (h + 1) * hw)
            w_ref[rs, cols] = wu[h * c:(h + 1) * c, :hw].astype(w_ref.dtype)
            u_ref[rs, cols] = wu[h * c:(h + 1) * c, hw:]
        kd_ref[rs, :] = (k[rs] * jnp.exp(g128[c - 1:c, :] - g128)).astype(kd_ref.dtype)


def _gdn_prep(qkv, conv_buf8, conv_w, gb):
    bsz, length, _ = qkv.shape
    c, nh, hw = GDN_CHUNK, GDN_HEADS, GDN_DIM
    rows = 4 * c if length % (4 * c) == 0 else c
    per = rows // SUBLANE
    src = jnp.arange(LANE)[:, None]

    def spread(width):
        dst = jnp.arange(2 * nh * width)[None, :]
        return (src == dst // width).astype(BF16)

    e64, e128 = spread(c), spread(hw)
    blk = jnp.arange(nh * c) // c
    bd_mask = (blk[:, None] == blk[None, :]).astype(BF16)
    cur = lambda w: pl.BlockSpec((None, rows, w), lambda b, i: (b, i, 0))
    prev = pl.BlockSpec((None, SUBLANE, CONV_CH), lambda b, i: (b, jnp.maximum(i * per - 1, 0), 0))
    buf = pl.BlockSpec((None, SUBLANE, CONV_CH), lambda b, i: (b, 0, 0))
    shapes = [(MIX_W, BF16), (MIX_W, F32), (MIX_W, BF16), (MIX_W, BF16),
              (GDN_HEADS * c, BF16), (MIX_W, F32)]
    return pl.pallas_call(
        _gdn_prep_body,
        grid=(bsz, length // rows),
        in_specs=[cur(CONV_CH), prev, buf, _full(conv_w.shape), cur(LANE),
                  _full(e64.shape), _full(e128.shape), _full(bd_mask.shape)],
        out_specs=[cur(w) for w, _ in shapes],
        out_shape=[jax.ShapeDtypeStruct((bsz, length, w), d) for w, d in shapes],
        compiler_params=_cparams(("parallel", "parallel")),
        name="gdn_prep",
    )(qkv, qkv, conv_buf8, conv_w, gb, e64, e128, bd_mask)


def _gdn_seq_body(w_ref, u_ref, kd_ref, q_ref, aqk_ref, eg_ref, szd_ref, s0_ref, ng_ref,
                  out_ref, s_ref):
    c = GDN_CHUNK
    i = pl.program_id(1)

    @pl.when(i == 0)
    def _():
        s_ref[...] = s0_ref[...]

    hw = GDN_DIM
    items = [(bb, h) for bb in range(s_ref.shape[0]) for h in range(GDN_HEADS)]
    cols = lambda h: slice(h * hw, (h + 1) * hw)
    s_bf = [s_ref[bb, h].astype(BF16) for bb, h in items]
    ws_qs = [_dot(jnp.concatenate([w_ref[bb, :, cols(h)], q_ref[bb, :, cols(h)]], axis=0), sb)
             for (bb, h), sb in zip(items, s_bf)]
    d_bf = [(u_ref[bb, :, cols(h)] - x[:c]).astype(BF16) for (bb, h), x in zip(items, ws_qs)]
    upd = [_dot(jnp.concatenate([aqk_ref[bb, :, h * c:(h + 1) * c], kd_ref[bb, :, cols(h)].T], axis=0),
                d) for (bb, h), d in zip(items, d_bf)]
    for (bb, h), y in zip(items, upd):
        s_ref[bb, h] = eg_ref[bb, c - 1:c, cols(h)] * s_ref[bb, h] + y[c:]
    o = [eg_ref[bb, :, cols(h)] * x[c:] + y[:c] for (bb, h), x, y in zip(items, ws_qs, upd)]
    ms = [jnp.mean(x * x, axis=-1, keepdims=True) for x in o]
    for (bb, h), x, m in zip(items, o, ms):
        out_ref[bb, :, cols(h)] = (x * lax.rsqrt(m + EPS) * ng_ref[...]
                                   * szd_ref[bb, :, cols(h)]).astype(out_ref.dtype)


def _gdn_seq(w, u, kd, q, aqk, eg, szd, s0, norm_g):
    bsz, length, _ = w.shape
    c = GDN_CHUNK
    nbb = 4
    cur = lambda width: pl.BlockSpec((nbb, c, width), lambda b, i: (b, i, 0))
    st = pl.BlockSpec((nbb, GDN_HEADS, GDN_DIM, GDN_DIM), lambda b, i: (b, 0, 0, 0))
    return pl.pallas_call(
        _gdn_seq_body,
        grid=(bsz // nbb, length // c),
        in_specs=[cur(MIX_W), cur(MIX_W), cur(MIX_W), cur(MIX_W), cur(GDN_HEADS * c), cur(MIX_W),
                  cur(MIX_W), st, _full((1, GDN_DIM))],
        out_specs=[cur(MIX_W), st],
        out_shape=[jax.ShapeDtypeStruct((bsz, length, MIX_W), BF16),
                   jax.ShapeDtypeStruct(s0.shape, F32)],
        compiler_params=_cparams(("parallel", "arbitrary")),
        name="gdn_seq",
    )(w, u, kd, q, aqk, eg, szd, s0, norm_g)


def _row_tile(n):
    for tm in (512, 256, 128, 64):
        if n % tm == 0:
            return tm
    raise ValueError(n)


def _trunk(x, p, past, wts, s5w):
    bsz, length, _ = x.shape
    n = bsz * length
    first = past is None
    chunk = min(length, MLP_CHUNK)
    tm = min(length, 64)

    outs = _even_in(x, wts["norm_g0"], wts["even_w_in"], wts["a_ln_g"], wts["a_ln_b"],
                    wts["a_w_s"][:, :chunk, :chunk], wts["a_b_s"][:, :chunk, :],
                    chunk=chunk, tm=max(tm, chunk), want_av=not first)
    a_out, ub_tm, szb_tm = outs[:3]
    a_v = None if first else outs[3][None]
    if first:
        x0r = jnp.zeros((bsz, S5_LANES), F32)
        x0i = x0r
    else:
        x0r = past["b_re"].reshape(bsz, S5_LANES)
        x0i = past["b_im"].reshape(bsz, S5_LANES)
    b_out_tm, xr, xi = _s5(ub_tm, szb_tm, x0r, x0i, *s5w, wts["b_D"], wts["b_glu_w"], t_steps=tm)
    h = _mix_out(a_out, b_out_tm, x, p, 0, wts["even_w_out"], wts["ple_norm_g0"], wts["ple_gate_w0"],
                 wts["ple_proj0"], None, tm=tm, b_time_major=True)
    h = h.reshape(n, D_MODEL)

    tm1 = _row_tile(n)
    q, k, v, szc, qkv, szd, gb, k4, v4 = _odd_in(h, wts["norm_g1"], wts["odd_w_main"],
                                                 wts["odd_w_ab"], wts["d_A_log"], wts["d_dt_bias"],
                                                 tm=min(tm1, 256))
    to3 = lambda a: a.reshape(bsz, length, a.shape[-1])
    q, k, v, szc, qkv, szd, gb = map(to3, (q, k, v, szc, qkv, szd, gb))
    if first:
        c_out = _sb_attention(q, k, v, szc)
        buf8 = jnp.zeros((bsz, SUBLANE, CONV_CH), F32)
        s0 = jnp.zeros((bsz, GDN_HEADS, GDN_DIM, GDN_DIM), F32)
    else:
        c_out = _sb_attention(q, k, v, szc, past["k_c"], past["v_c"])
        buf8 = jnp.pad(past["conv_d"], ((0, 0), (SUBLANE - (CONV_W - 1), 0), (0, 0)))
        s0 = past["s_d"]
    w_, u_, kd_, qd_, aqk_, eg_ = _gdn_prep(qkv, buf8, wts["d_conv_w"], gb)
    d_out, s_new = _gdn_seq(w_, u_, kd_, qd_, aqk_, eg_, szd, s0, wts["d_norm_g"])
    y = _mix_out(c_out, d_out, h.reshape(bsz, length, D_MODEL), p, 1, wts["odd_w_out"],
                 wts["ple_norm_g1"], wts["ple_gate_w1"], wts["ple_proj1"], wts["final_norm_g"],
                 tm=tm, b_time_major=False)

    conv_new = qkv[:, length - (CONV_W - 1):, :]
    return (y,
            xr.reshape(1, bsz, S5_GROUPS, S5_N), xi.reshape(1, bsz, S5_GROUPS, S5_N), a_v,
            k4.reshape(1, bsz, length, SB_HEADS, SB_DIM), v4.reshape(1, bsz, length, SB_HEADS, SB_DIM),
            s_new[None], conv_new[None])


def kernel(x_prompt, x_sample, state_b_re, state_b_im, cache_k_c, cache_v_c, state_d, state_conv_d,
           p_prompt, p_sample,
           norm_g, final_norm_g, ple_proj, ple_gate_w, ple_norm_g,
           even_w_in, even_w_out, a_ln_g, a_ln_b, a_w_s, a_b_s,
           b_lam_re, b_lam_im, b_log_dt, b_B_re, b_B_im, b_C_re, b_C_im, b_D, b_glu_w,
           odd_w_in, odd_w_out, d_conv_w, d_A_log, d_dt_bias, d_norm_g):
    row = lambda a: a.reshape(1, -1).astype(F32)
    tril = jnp.tril(jnp.ones((MLP_CHUNK, MLP_CHUNK), dtype=bool))
    n_main = 8 * MIX_W
    pad_lane = lambda a: jnp.pad(a.reshape(1, -1).astype(F32), ((0, 0), (0, LANE - a.size)))
    wts = {
        "norm_g0": row(norm_g[0]), "norm_g1": row(norm_g[1]), "final_norm_g": row(final_norm_g),
        "ple_norm_g0": row(ple_norm_g[0]), "ple_norm_g1": row(ple_norm_g[1]),
        "ple_gate_w0": ple_gate_w[0].astype(BF16), "ple_gate_w1": ple_gate_w[1].astype(BF16),
        "ple_proj0": ple_proj[0].astype(BF16), "ple_proj1": ple_proj[1].astype(BF16),
        "even_w_in": even_w_in[0].astype(BF16), "even_w_out": even_w_out[0].astype(BF16),
        "a_ln_g": row(a_ln_g[0]), "a_ln_b": row(a_ln_b[0]),
        "a_w_s": jnp.where(tril, a_w_s[0], 0.0).astype(BF16),
        "a_b_s": jnp.broadcast_to(a_b_s[0][:, :, None], (A_GROUPS, MLP_CHUNK, LANE)).astype(F32),
        "b_D": row(b_D[0]), "b_glu_w": b_glu_w[0].astype(BF16),
        "odd_w_main": odd_w_in[0][:, :n_main].astype(BF16),
        "odd_w_ab": jnp.pad(odd_w_in[0][:, n_main:], ((0, 0), (0, LANE - 2 * GDN_HEADS))).astype(BF16),
        "odd_w_out": odd_w_out[0].astype(BF16),
        "d_conv_w": d_conv_w[0].astype(F32),
        "d_A_log": pad_lane(d_A_log[0]), "d_dt_bias": pad_lane(d_dt_bias[0]),
        "d_norm_g": row(d_norm_g[0]),
    }
    ar, ai, bbr, bbi = _s5_disc(b_lam_re[0], b_lam_im[0], b_log_dt[0], b_B_re[0], b_B_im[0])
    bre, bim, cre, cim = _s5_block_diag((bbr, bbi), b_C_re[0], b_C_im[0])
    s5w = (ar.reshape(1, S5_LANES), ai.reshape(1, S5_LANES), bre, bim, cre, cim)

    (y_p, b_re_p, b_im_p, _, k_p, v_p, s_p, conv_p) = _trunk(x_prompt, p_prompt, None, wts, s5w)
    past = {"b_re": state_b_re[0], "b_im": state_b_im[0],
            "k_c": jnp.transpose(cache_k_c[0], (0, 2, 3, 1)),
            "v_c": jnp.transpose(cache_v_c[0], (0, 2, 3, 1)),
            "s_d": state_d[0], "conv_d": state_conv_d[0]}
    (y_s, b_re_s, b_im_s, a_v_s, k_s, v_s, s_s, conv_s) = _trunk(x_sample, p_sample, past, wts, s5w)
    return (y_p, y_s, b_re_p, b_im_p, k_p, v_p, s_p, conv_p,
            b_re_s, b_im_s, a_v_s, k_s, v_s, s_s, conv_s)
```

```python
import functools
import math

import jax
import jax.numpy as jnp
from jax import lax
from jax.experimental import pallas as pl
from jax.experimental.pallas import tpu as pltpu

F32 = jnp.float32
BF16 = jnp.bfloat16
EPS = 1e-6

D_MODEL = 1024
PLE_DIM = 256
MIX_W = 512
A_GROUPS = 4
MLP_CHUNK = 128
S5_GROUPS = 32
S5_P = 16
S5_N = 64
S5_LANES = S5_GROUPS * S5_N
SB_HEADS = 8
SB_DIM = 64
SB_BLOCK = 128
GDN_HEADS = 4
GDN_DIM = 128
GDN_CHUNK = 64
CONV_W = 4
CONV_CH = 3 * MIX_W
LANE = 128
SUBLANE = 8
VMEM_LIMIT = 52 * 1024 * 1024
SB_DEAD = -104.0


def _cparams(sem):
    return pltpu.CompilerParams(dimension_semantics=sem, vmem_limit_bytes=VMEM_LIMIT)


def _gelu(x):
    return 0.5 * x * (1.0 + jnp.tanh(0.7978845608028654 * (x + 0.044715 * (x * x * x))))


def _sigmoid(x):
    return 1.0 / (1.0 + jnp.exp(-x))


def _silu(x):
    return x * _sigmoid(x)


def _softplus(x):
    return jnp.maximum(x, 0.0) + jnp.log1p(jnp.exp(-jnp.abs(x)))


def _rms(x, g):
    ms = jnp.mean(x * x, axis=-1, keepdims=True)
    return x * lax.rsqrt(ms + EPS) * g


def _dot(a, b):
    return jnp.dot(a, b, preferred_element_type=F32)


def _dot_nt(a, b):
    return lax.dot_general(a, b, (((1,), (1,)), ((), ())), preferred_element_type=F32)


def _dot_tn(a, b):
    return lax.dot_general(a, b, (((0,), (0,)), ((), ())), preferred_element_type=F32)


def _split3(x):
    hi = x.astype(BF16)
    r = x - hi.astype(F32)
    mid = r.astype(BF16)
    lo = (r - mid.astype(F32)).astype(BF16)
    return hi, mid, lo


def _full(shape):
    n = len(shape)
    return pl.BlockSpec(shape, lambda *_: (0,) * n)


def _time_major(x, nseq):
    tm = x.shape[0] // nseq
    return pltpu.einshape("btd->tbd", x.reshape(nseq, tm, x.shape[1]))


def _even_in_body(x_ref, g_ref, w_ref, lng_ref, lnb_ref, ws_ref, bs_ref,
                  aout_ref, ub_ref, szb_ref, av_ref, *, chunk):
    nseq, tm, _ = x_ref.shape
    rows_all = nseq * tm
    hn = _rms(x_ref[...].reshape(rows_all, D_MODEL), g_ref[...]).astype(BF16)

    def proj(k):
        return _dot(hn, w_ref[:, k * MIX_W:(k + 1) * MIX_W])

    va = _gelu(proj(1))
    mu = jnp.mean(va, axis=-1, keepdims=True)
    vc = va - mu
    var = jnp.mean(vc * vc, axis=-1, keepdims=True)
    va = vc * lax.rsqrt(var + EPS) * lng_ref[...] + lnb_ref[...]
    if av_ref is not None:
        av_ref[...] = va.reshape(av_ref.shape)
    va_bf = va.astype(BF16)
    gate = _gelu(proj(0)) * _silu(proj(2))
    gw = MIX_W // A_GROUPS
    pieces = []
    for ch in range(rows_all // chunk):
        rows = slice(ch * chunk, (ch + 1) * chunk)
        s = [_dot(ws_ref[g], va_bf[rows, g * gw:(g + 1) * gw]) + bs_ref[g] for g in range(A_GROUPS)]
        pieces.append(gate[rows] * jnp.concatenate(s, axis=1))
    aout_ref[...] = jnp.concatenate(pieces, axis=0).astype(aout_ref.dtype).reshape(aout_ref.shape)
    ub_ref[...] = _time_major(proj(3), nseq)
    szb_ref[...] = _time_major(_silu(proj(4)), nseq)


def _even_in(x, norm_g, w_in, ln_g, ln_b, w_s, b_s, *, chunk, tm, want_av):
    bsz, length, _ = x.shape
    body = functools.partial(_even_in_body, chunk=chunk)
    if not want_av:
        body_fn = lambda *r: body(*r, None)
    else:
        body_fn = body
    seq = lambda w: pl.BlockSpec((SUBLANE, tm, w), lambda b, i: (b, i, 0))
    tmaj = pl.BlockSpec((tm, SUBLANE, MIX_W), lambda b, i: (i, b, 0))
    out_shape = [jax.ShapeDtypeStruct((bsz, length, MIX_W), BF16),
                 jax.ShapeDtypeStruct((length, bsz, MIX_W), F32),
                 jax.ShapeDtypeStruct((length, bsz, MIX_W), F32)]
    out_specs = [seq(MIX_W), tmaj, tmaj]
    if want_av:
        out_shape.append(jax.ShapeDtypeStruct((bsz, length, MIX_W), F32))
        out_specs.append(seq(MIX_W))
    return pl.pallas_call(
        body_fn,
        grid=(bsz // SUBLANE, length // tm),
        in_specs=[seq(D_MODEL),
                  _full((1, D_MODEL)),
                  _full(w_in.shape),
                  _full((1, MIX_W)), _full((1, MIX_W)),
                  _full(w_s.shape), _full(b_s.shape)],
        out_specs=out_specs,
        out_shape=out_shape,
        compiler_params=_cparams(("parallel", "parallel")),
        name="even_in",
    )(x, norm_g, w_in, ln_g, ln_b, w_s, b_s)


def _s5_disc_body(lr_ref, li_ref, ldt_ref, br_ref, bi_ref, ar_ref, ai_ref, bbr_ref, bbi_ref):
    dt = jnp.exp(ldt_ref[...])
    lr, li = lr_ref[...], li_ref[...]
    mag = jnp.exp(lr * dt)
    ar = mag * jnp.cos(li * dt)
    ai = mag * jnp.sin(li * dt)
    den = lr * lr + li * li
    cr = ((ar - 1.0) * lr + ai * li) / den
    ci = (ai * lr - (ar - 1.0) * li) / den
    ar_ref[...] = ar
    ai_ref[...] = ai
    for p in range(S5_P):
        br, bi = br_ref[p], bi_ref[p]
        bbr_ref[p] = cr * br - ci * bi
        bbi_ref[p] = cr * bi + ci * br


def _s5_disc(lam_re, lam_im, log_dt, b_re, b_im):
    gn = (S5_GROUPS, S5_N)
    pgn = (S5_P, S5_GROUPS, S5_N)
    return pl.pallas_call(
        _s5_disc_body,
        out_shape=[jax.ShapeDtypeStruct(gn, F32), jax.ShapeDtypeStruct(gn, F32),
                   jax.ShapeDtypeStruct(pgn, F32), jax.ShapeDtypeStruct(pgn, F32)],
        name="s5_disc",
    )(lam_re, lam_im, log_dt.reshape(S5_GROUPS, 1),
      jnp.transpose(b_re, (2, 0, 1)), jnp.transpose(b_im, (2, 0, 1)))


def _s5_block_diag(bb_pgn, c_re, c_im):
    gpc = LANE // S5_P
    nj = S5_GROUPS // gpc
    eye = jnp.eye(gpc, dtype=F32)

    def b_blk(bb):
        b = jnp.transpose(bb, (1, 0, 2)).reshape(nj, gpc, S5_P, S5_N)
        m = b[:, :, :, None, :] * eye[None, :, None, :, None]
        return m.reshape(nj, gpc * S5_P, gpc * S5_N).astype(BF16)

    def c_blk(c):
        cc = jnp.transpose(c, (0, 2, 1)).reshape(nj, gpc, S5_N, S5_P)
        m = cc[:, :, :, None, :] * eye[None, :, None, :, None]
        return m.reshape(nj, gpc * S5_N, gpc * S5_P).astype(BF16)

    return b_blk(bb_pgn[0]), b_blk(bb_pgn[1]), c_blk(c_re), c_blk(-c_im)


def _s5_body(u_ref, szb_ref, x0r_ref, x0i_ref, ar_ref, ai_ref, bre_ref, bim_ref, cre_ref, cim_ref,
             d_ref, glu_ref, out_ref, st_r, st_i, sre, sim):
    t_steps = u_ref.shape[0]
    rows = t_steps * SUBLANE
    ti = pl.program_id(1)
    cw = bre_ref.shape[2]
    nj = bre_ref.shape[0]

    @pl.when(ti == 0)
    def _():
        st_r[...] = x0r_ref[...]
        st_i[...] = x0i_ref[...]

    u = u_ref[...].reshape(rows, MIX_W)
    u_bf = u.astype(BF16)
    for j in range(nj):
        uj = u_bf[:, j * LANE:(j + 1) * LANE]
        sre[:, j * cw:(j + 1) * cw] = _dot(uj, bre_ref[j])
        sim[:, j * cw:(j + 1) * cw] = _dot(uj, bim_ref[j])

    for j in range(nj):
        lanes = slice(j * cw, (j + 1) * cw)
        a_r = jnp.broadcast_to(ar_ref[:, lanes], (SUBLANE, cw))
        a_i = jnp.broadcast_to(ai_ref[:, lanes], (SUBLANE, cw))

        def step(t, carry, lanes=lanes, a_r=a_r, a_i=a_i):
            xr, xi = carry
            r0 = pl.multiple_of(t * SUBLANE, SUBLANE)
            nxr = a_r * xr - a_i * xi + sre[pl.ds(r0, SUBLANE), lanes]
            nxi = a_r * xi + a_i * xr + sim[pl.ds(r0, SUBLANE), lanes]
            sre[pl.ds(r0, SUBLANE), lanes] = nxr
            sim[pl.ds(r0, SUBLANE), lanes] = nxi
            return nxr, nxi

        xr, xi = lax.fori_loop(0, t_steps, step, (st_r[:, lanes], st_i[:, lanes]), unroll=8)
        st_r[:, lanes] = xr
        st_i[:, lanes] = xi

    ys = []
    for j in range(nj):
        lanes = slice(j * cw, (j + 1) * cw)
        ys.append(_dot(sre[:, lanes].astype(BF16), cre_ref[j])
                  + _dot(sim[:, lanes].astype(BF16), cim_ref[j]))
    y = jnp.concatenate(ys, axis=1) + d_ref[...] * u
    y = _gelu(y)
    y = y * _sigmoid(_dot(y.astype(BF16), glu_ref[...]))
    y = y * szb_ref[...].reshape(rows, MIX_W)
    out_ref[...] = y.reshape(out_ref.shape)


def _s5(u_tm, szb_tm, x0r, x0i, ar, ai, bre, bim, cre, cim, d_skip, glu_w, *, t_steps):
    length, nb, _ = u_tm.shape
    blk = pl.BlockSpec((t_steps, SUBLANE, MIX_W), lambda b, t: (t, b, 0))
    st = pl.BlockSpec((SUBLANE, S5_LANES), lambda b, t: (b, 0))
    rows = t_steps * SUBLANE
    return pl.pallas_call(
        _s5_body,
        grid=(nb // SUBLANE, length // t_steps),
        in_specs=[blk, blk, st, st, _full((1, S5_LANES)), _full((1, S5_LANES)),
                  _full(bre.shape), _full(bim.shape), _full(cre.shape), _full(cim.shape),
                  _full((1, MIX_W)), _full(glu_w.shape)],
        out_specs=[blk, st, st],
        out_shape=[jax.ShapeDtypeStruct((length, nb, MIX_W), F32),
                   jax.ShapeDtypeStruct((nb, S5_LANES), F32),
                   jax.ShapeDtypeStruct((nb, S5_LANES), F32)],
        scratch_shapes=[pltpu.VMEM((rows, S5_LANES), F32), pltpu.VMEM((rows, S5_LANES), F32)],
        compiler_params=_cparams(("parallel", "arbitrary")),
        name="s5_scan",
    )(u_tm, szb_tm, x0r, x0i, ar, ai, bre, bim, cre, cim, d_skip, glu_w)


def _mix_out_body(a_ref, b_ref, h_ref, p_ref, wout_ref, pg_ref, wgate_ref, wp_ref, fg_ref, out_ref,
                  *, b_time_major):
    rows = out_ref.shape[0] * out_ref.shape[1]
    flat = lambda ref: ref[...].reshape(rows, ref.shape[-1])
    if b_time_major:
        b = jnp.concatenate([b_ref[:, s, :] for s in range(b_ref.shape[1])], axis=0)
    else:
        b = flat(b_ref)
    mix = _dot(flat(a_ref), wout_ref[:MIX_W, :]) + _dot(b.astype(BF16), wout_ref[MIX_W:, :])
    h1 = flat(h_ref) + mix
    gate = _sigmoid(_dot(_rms(h1, pg_ref[...]).astype(BF16), wgate_ref[...]))
    h2 = h1 + gate * _dot(flat(p_ref).astype(BF16), wp_ref[...])
    if fg_ref is not None:
        h2 = _rms(h2, fg_ref[...])
    out_ref[...] = h2.reshape(out_ref.shape)


def _mix_out(a, b, h, p, layer, w_out, ple_g, w_gate, w_p, final_g, *, tm, b_time_major):
    bsz, length, _ = h.shape
    seq = lambda w: pl.BlockSpec((SUBLANE, tm, w), lambda s, i: (s, i, 0))
    b_spec = pl.BlockSpec((tm, SUBLANE, MIX_W), lambda s, i: (i, s, 0)) if b_time_major else seq(MIX_W)
    args = [a, b, h, p, w_out, ple_g, w_gate, w_p]
    in_specs = [seq(MIX_W), b_spec, seq(D_MODEL),
                pl.BlockSpec((None, SUBLANE, tm, PLE_DIM), lambda s, i: (layer, s, i, 0)),
                _full(w_out.shape), _full((1, D_MODEL)), _full(w_gate.shape), _full(w_p.shape)]
    body = functools.partial(_mix_out_body, b_time_major=b_time_major)
    if final_g is None:
        body_fn = lambda *r: body(*r[:8], None, r[8])
    else:
        body_fn = body
        args.append(final_g)
        in_specs.append(_full((1, D_MODEL)))
    return pl.pallas_call(
        body_fn,
        grid=(bsz // SUBLANE, length // tm),
        in_specs=in_specs,
        out_specs=seq(D_MODEL),
        out_shape=jax.ShapeDtypeStruct((bsz, length, D_MODEL), F32),
        compiler_params=_cparams(("parallel", "parallel")),
        name="mix_out",
    )(*args)


def _odd_in_body(h_ref, g_ref, w_ref, wab_ref, alog_ref, dtb_ref,
                 q_ref, k_ref, v_ref, szc_ref, qkv_ref, szd_ref, gb_ref, k4_ref, v4_ref):
    hn = _rms(h_ref[...], g_ref[...]).astype(BF16)

    def proj(k, width=MIX_W):
        return _dot(hn, w_ref[:, k * MIX_W:k * MIX_W + width])

    def head_major(x):
        return pltpu.einshape("t(hd)->thd", x, h=SB_HEADS)

    q_ref[...] = (proj(0) * (SB_DIM ** -0.5)).astype(BF16)
    kc, vc = proj(1), proj(2)
    k_ref[...] = kc.astype(BF16)
    v_ref[...] = vc.astype(BF16)
    k4_ref[...] = head_major(kc)
    v4_ref[...] = head_major(vc)
    szc_ref[...] = _silu(proj(3))
    for k in range(3):
        qkv_ref[:, k * MIX_W:(k + 1) * MIX_W] = proj(4 + k)
    szd_ref[...] = _silu(proj(7))
    ab = _dot(hn, wab_ref[...])
    g = -jnp.exp(alog_ref[...]) * _softplus(ab + dtb_ref[...])
    lane = lax.broadcasted_iota(jnp.int32, ab.shape, 1)
    gb_ref[...] = jnp.where(lane < GDN_HEADS, g, _sigmoid(ab))


def _odd_in(h2d, norm_g, w_main, w_ab, a_log, dt_bias, *, tm):
    n = h2d.shape[0]
    row = lambda i: (i, 0)
    widths = [MIX_W, MIX_W, MIX_W, MIX_W, CONV_CH, MIX_W, LANE]
    dtypes = [BF16, BF16, BF16, F32, F32, F32, F32]
    head4 = pl.BlockSpec((tm, SB_HEADS, SB_DIM), lambda i: (i, 0, 0))
    head4_shape = jax.ShapeDtypeStruct((n, SB_HEADS, SB_DIM), F32)
    return pl.pallas_call(
        _odd_in_body,
        grid=(n // tm,),
        in_specs=[pl.BlockSpec((tm, D_MODEL), row), _full((1, D_MODEL)),
                  _full(w_main.shape), _full(w_ab.shape), _full((1, LANE)), _full((1, LANE))],
        out_specs=[pl.BlockSpec((tm, w), row) for w in widths] + [head4, head4],
        out_shape=[jax.ShapeDtypeStruct((n, w), d) for w, d in zip(widths, dtypes)]
        + [head4_shape, head4_shape],
        compiler_params=_cparams(("parallel",)),
        name="odd_in",
    )(h2d, norm_g, w_main, w_ab, a_log, dt_bias)


def _sb_pairs(q2x, k_blocks, v_blocks, masks, valids, c_in, cum_rhs, key_major):
    npair = len(q2x)
    nblk = len(k_blocks[0])
    rows = q2x[0].shape[0]
    if key_major:
        z = [_dot(q2x[p], jnp.concatenate(k_blocks[p], axis=1)) for p in range(npair)]
    else:
        z = [_dot_nt(q2x[p], jnp.concatenate(k_blocks[p], axis=0)) for p in range(npair)]
    log_beta, hilo = [], []
    for p in range(npair):
        lb, hl = [], []
        for j in range(nblk):
            zj = z[p][:, j * LANE:(j + 1) * LANE]
            l1p = jnp.log(1.0 + jnp.exp(-jnp.abs(zj)))
            lb.append(jnp.minimum(zj, 0.0) - l1p)
            log_keep = -jnp.maximum(zj, 0.0) - l1p
            if masks[j] is not None:
                log_keep = jnp.where(masks[j], log_keep, 0.0)
            if valids[j] is not None:
                log_keep = jnp.where(valids[j], log_keep, 0.0)
            hi = log_keep.astype(BF16)
            lo = (log_keep - hi.astype(F32)).astype(BF16)
            hl.append(jnp.concatenate([hi, lo], axis=1))
        log_beta.append(lb)
        hilo.append(jnp.concatenate(hl, axis=0))
    cum = [_dot(x, cum_rhs) for x in hilo]
    ws, totals = [], []
    for p in range(npair):
        c = c_in[p]
        w_p = [None] * nblk
        for j in reversed(range(nblk)):
            after = cum[p][j * rows:(j + 1) * rows, :LANE]
            if c is not None:
                after = after + c
            w = jnp.exp(log_beta[p][j] + after)
            if masks[j] is not None:
                w = jnp.where(masks[j], w, 0.0)
            if valids[j] is not None:
                w = jnp.where(valids[j], w, 0.0)
            w_p[j] = w.astype(BF16)
            total = cum[p][j * rows:(j + 1) * rows, LANE:]
            c = total if c is None else c + total
        ws.append(jnp.concatenate(w_p, axis=1))
        totals.append(c)
    if key_major:
        pv = [_dot_nt(ws[p], jnp.concatenate(v_blocks[p], axis=1)) for p in range(npair)]
    else:
        pv = [_dot(ws[p], jnp.concatenate(v_blocks[p], axis=0)) for p in range(npair)]
    return pv, totals


def _sb_body(q_ref, kd_ref, vd_ref, k1_ref, v1_ref, k2_ref, v2_ref, szc_ref, cum_ref,
             kold_ref, vold_ref, out_ref, acc_ref, c_ref, kbuf, vbuf, sem, *, top_of, key_major):
    b = pl.program_id(0)
    i = pl.program_id(1)
    tq = q_ref.shape[0]
    tk = kbuf.shape[-1] if key_major else kbuf.shape[0]
    npair = MIX_W // LANE
    pairs = range(npair)
    tt = lax.broadcasted_iota(jnp.int32, (2 * tq, tk), 0)
    ss = lax.broadcasted_iota(jnp.int32, (2 * tq, tk), 1)
    diag_mask = ss < jnp.where(tt >= tq, tt - tq, tt)
    even_lane = lax.broadcasted_iota(jnp.int32, (tq, LANE), 1) < SB_DIM
    top = top_of(i)

    def diag_tile(ref, p):
        x = ref[:, p * LANE:(p + 1) * LANE]
        if key_major:
            x = x.astype(F32).T
            if tq < tk:
                x = jnp.concatenate([x, jnp.zeros((LANE, tk - tq), F32)], axis=1)
        elif tq < tk:
            x = jnp.concatenate([x, jnp.zeros((tk - tq, LANE), x.dtype)], axis=0)
        return x.astype(BF16)

    def old_tile(ref, p):
        if key_major:
            return jnp.concatenate([ref[2 * p], ref[2 * p + 1]], axis=0).astype(BF16)
        return ref[:, p * LANE:(p + 1) * LANE].astype(BF16)

    def q_pair(p):
        q2 = q_ref[:, p * LANE:(p + 1) * LANE]
        zero = jnp.zeros_like(q2)
        return jnp.concatenate([jnp.where(even_lane, q2, zero), jnp.where(even_lane, zero, q2)], axis=0)

    def merge(pv):
        return jnp.where(even_lane, pv[:tq], pv[tq:])

    q2x = [q_pair(p) for p in pairs]
    k_win = [[old_tile(k2_ref, p), old_tile(k1_ref, p), diag_tile(kd_ref, p)] for p in pairs]
    v_win = [[old_tile(v2_ref, p), old_tile(v1_ref, p), diag_tile(vd_ref, p)] for p in pairs]
    masks = [None, None, diag_mask]
    valids = [top >= 1, top >= 0, None]
    cum_rhs = cum_ref[...]
    pvs, sums = _sb_pairs(q2x, k_win, v_win, masks, valids, [None] * npair, cum_rhs, key_major)
    acc_all = jnp.concatenate([merge(pv) for pv in pvs], axis=1)
    out_ref[...] = (acc_all * szc_ref[...]).astype(out_ref.dtype)
    c_max = functools.reduce(jnp.maximum, sums)

    @pl.when(jnp.logical_and(top >= 2, jnp.max(c_max) > SB_DEAD))
    def _():
        acc_ref[...] = acc_all
        for p in pairs:
            c_ref[p] = sums[p]

        def cond(carry):
            j, go = carry
            return jnp.logical_and(j >= 0, go)

        def body(carry):
            j, _ = carry
            r0 = pl.multiple_of(j * tk, tk)
            if key_major:
                src = lambda ref: ref.at[b, :, :, pl.ds(r0, tk)]
            else:
                src = lambda ref: ref.at[b, pl.ds(r0, tk)]
            ck = pltpu.make_async_copy(src(kold_ref), kbuf, sem.at[0])
            cv = pltpu.make_async_copy(src(vold_ref), vbuf, sem.at[1])
            ck.start()
            cv.start()
            ck.wait()
            cv.wait()
            pvs, cs = _sb_pairs(q2x, [[old_tile(kbuf, p)] for p in pairs],
                                [[old_tile(vbuf, p)] for p in pairs], [None], [None],
                                [c_ref[p] for p in pairs], cum_rhs, key_major)
            for p in pairs:
                acc_ref[:, p * LANE:(p + 1) * LANE] += merge(pvs[p])
                c_ref[p] = cs[p]
            return j - 1, jnp.max(c_ref[...]) > SB_DEAD

        lax.while_loop(cond, body, (top - 2, jnp.max(c_max) > SB_DEAD))
        out_ref[...] = (acc_ref[...] * szc_ref[...]).astype(out_ref.dtype)


def _sb_attention(q, k_new, v_new, szc, k_past=None, v_past=None):
    bsz, length, _ = q.shape
    tq = min(length, SB_BLOCK)
    nq = length // tq
    tk = SB_BLOCK
    key_major = k_past is not None
    if not key_major:
        assert tq == tk
        k_old, v_old = k_new, v_new
        top_of = lambda i: i - 1
        old_blk = (tk, MIX_W)
        old_idx = lambda b, blk: (b, blk, 0)
    else:
        assert nq == 1 and k_past.shape[-1] % tk == 0
        k_old, v_old = k_past, v_past
        n_old = k_past.shape[-1] // tk
        top_of = lambda i: i * 0 + (n_old - 1)
        old_blk = (SB_HEADS, SB_DIM, tk)
        old_idx = lambda b, blk: (b, 0, 0, blk)
    cur = pl.BlockSpec((None, tq, MIX_W), lambda b, i: (b, i, 0))
    old1 = pl.BlockSpec((None,) + old_blk, lambda b, i: old_idx(b, jnp.maximum(top_of(i), 0)))
    old2 = pl.BlockSpec((None,) + old_blk, lambda b, i: old_idx(b, jnp.maximum(top_of(i) - 1, 0)))
    anyspec = pl.BlockSpec(memory_space=pl.ANY)
    jj = lax.broadcasted_iota(jnp.int32, (tk, tk), 0)
    ss = lax.broadcasted_iota(jnp.int32, (tk, tk), 1)
    half = jnp.concatenate([(jj > ss).astype(BF16), jnp.ones((tk, LANE), BF16)], axis=1)
    cum_rhs = jnp.concatenate([half, half], axis=0)
    return pl.pallas_call(
        functools.partial(_sb_body, top_of=top_of, key_major=key_major),
        grid=(bsz, nq),
        in_specs=[cur, cur, cur, old1, old1, old2, old2, cur, _full(cum_rhs.shape),
                  anyspec, anyspec],
        out_specs=cur,
        out_shape=jax.ShapeDtypeStruct((bsz, length, MIX_W), BF16),
        scratch_shapes=[pltpu.VMEM((tq, MIX_W), F32),
                        pltpu.VMEM((MIX_W // LANE, 2 * tq, LANE), F32),
                        pltpu.VMEM(old_blk, k_old.dtype), pltpu.VMEM(old_blk, v_old.dtype),
                        pltpu.SemaphoreType.DMA((2,))],
        compiler_params=_cparams(("parallel", "parallel")),
        name="sb_attention",
    )(q, k_new, v_new, k_old, v_old, k_old, v_old, szc, cum_rhs, k_old, v_old)


def _dot01(m01, x):
    hi, mid, lo = _split3(x)
    return _dot(m01, hi) + _dot(m01, mid) + _dot(m01, lo)


def _dot01_r(x, m01):
    hi, mid, lo = _split3(x)
    return _dot(hi, m01) + _dot(mid, m01) + _dot(lo, m01)


def _gdn_body(cur_ref, prev_ref, buf_ref, gb_ref, szd_ref, s0_ref, cw_ref, e64_ref, e128_ref,
              bd_ref, ng_ref, out_ref, s_ref):
    c = GDN_CHUNK
    nseq = cur_ref.shape[0]
    rows = nseq * c
    nh, hw = GDN_HEADS, GDN_DIM
    hc = nh * c
    sh = int(math.log2(c))
    first = pl.program_id(1) == 0
    chunk_rows = [slice(s * c, (s + 1) * c) for s in range(nseq)]

    @pl.when(first)
    def _():
        s_ref[...] = s0_ref[...]

    def conv_silu(part):
        cols = slice(part * MIX_W, (part + 1) * MIX_W)
        cur = cur_ref[:, :, cols].reshape(rows, MIX_W)
        cw = cw_ref[:, cols]

        def conv(shifted):
            acc = shifted(CONV_W - 1) * cw[0:1, :]
            for t in range(1, CONV_W):
                acc = acc + shifted(CONV_W - 1 - t) * cw[t:t + 1, :]
            return acc

        y = conv(lambda d: pltpu.roll(cur, d, axis=0) if d else cur)
        pieces = []
        for s, rs in enumerate(chunk_rows):
            prev = jnp.where(first, buf_ref[s, :, cols], prev_ref[s, :, cols])
            head = jnp.concatenate([prev, cur[rs.start:rs.start + SUBLANE]], axis=0)
            pieces.append(conv(lambda d, head=head: head[SUBLANE - d:2 * SUBLANE - d]))
            pieces.append(y[rs.start + SUBLANE:rs.stop])
        return _silu(jnp.concatenate(pieces, axis=0))

    def l2n(x, scale):
        parts = []
        for h in range(nh):
            xh = x[:, h * hw:(h + 1) * hw]
            parts.append(xh * (lax.rsqrt(jnp.sum(xh * xh, axis=-1, keepdims=True) + EPS) * scale))
        return jnp.concatenate(parts, axis=1)

    k = l2n(conv_silu(1), 1.0)
    k_bf = k.astype(BF16)

    gb = gb_ref[...].reshape(rows, LANE)
    r_i = lax.broadcasted_iota(jnp.int32, (rows, rows), 0)
    c_i = lax.broadcasted_iota(jnp.int32, (rows, rows), 1)
    same_chunk = lax.shift_right_logical(r_i, sh) == lax.shift_right_logical(c_i, sh)
    g_all = _dot01(jnp.logical_and(same_chunk, c_i <= r_i).astype(BF16), gb)
    lane = lax.broadcasted_iota(jnp.int32, gb.shape, 1)
    x = jnp.where(lane < nh, g_all, gb)
    s64 = _dot01_r(x, e64_ref[...])
    s128 = _dot01_r(x, e128_ref[...])
    eg_all = jnp.exp(s128[:, :MIX_W])

    t_i = lax.broadcasted_iota(jnp.int32, (c, hc), 0)
    s_i = jnp.bitwise_and(lax.broadcasted_iota(jnp.int32, (c, hc), 1), c - 1)
    causal = s_i <= t_i
    strict = s_i < t_i
    eye = s_i == t_i
    blk_of_lane = lax.shift_right_logical(lax.broadcasted_iota(jnp.int32, (c, hc), 1), sh)
    bd_mask = bd_ref[...]

    def block_diag(x_bf):
        return jnp.concatenate([x_bf] * nh, axis=0) * bd_mask

    def diag_blocks(p):
        out = None
        for h in range(nh):
            sel = jnp.where(blk_of_lane == h, p[h * c:(h + 1) * c, :], 0.0)
            out = sel if out is None else out + sel
        return out

    def heads_on_rows(x_bf):
        return jnp.concatenate([x_bf[:, h * hw:(h + 1) * hw] for h in range(nh)], axis=0)

    tinv, pw, decay, k_rows = [], [], [], []
    for rs in chunk_rows:
        g64, b64 = s64[rs, :hc], s64[rs, hc:]
        g_row = jnp.sum(jnp.where(eye, g64, 0.0), axis=0, keepdims=True)
        decay.append(jnp.exp(jnp.where(causal, g64 - g_row, -1e30)))
        k_rows.append(heads_on_rows(k_bf[rs]))
        nmat = jnp.where(strict, b64 * decay[-1] * diag_blocks(_dot_nt(k_rows[-1], k_rows[-1])), 0.0)
        tinv.append(jnp.where(eye, 1.0, 0.0) - nmat)
        pw.append(nmat.astype(BF16))

    state = {}

    def fill_q():
        state["q"] = l2n(conv_silu(0), hw ** -0.5).astype(BF16)

    def fill_qk():
        state["aqk"] = []
        for rs, kr, dc in zip(chunk_rows, k_rows, decay):
            qk = diag_blocks(_dot_nt(heads_on_rows(state["q"][rs]), kr))
            state["aqk"].append(jnp.where(causal, qk * dc, 0.0).astype(BF16))

    def fill_v():
        state["v"] = conv_silu(2)

    def fill_rhs():
        state["rhs"] = []
        for rs in chunk_rows:
            b128 = s128[rs, MIX_W:]
            bk = (b128 * eg_all[rs] * k[rs]).astype(BF16)
            bv = (b128 * state["v"][rs]).astype(BF16)
            state["rhs"].append(jnp.concatenate([heads_on_rows(bk), heads_on_rows(bv)], axis=1))

    def fill_kd():
        state["kd"] = []
        for rs in chunk_rows:
            g128 = s128[rs, :MIX_W]
            state["kd"].append((k[rs] * jnp.exp(g128[c - 1:c, :] - g128)).astype(BF16))

    fillers = [fill_q, fill_qk, fill_v, fill_rhs, fill_kd]
    assert len(fillers) == sh - 1
    for level in range(sh - 1):
        pw = [_dot(p, block_diag(p)).astype(BF16) for p in pw]
        fillers[level]()
        tinv = [t + _dot(t.astype(BF16), block_diag(p)) for t, p in zip(tinv, pw)]
    wu = [_dot(block_diag(t.astype(BF16)), rhs) for t, rhs in zip(tinv, state["rhs"])]

    items = [(s, h) for s in range(nseq) for h in range(nh)]
    cols = lambda h: slice(h * hw, (h + 1) * hw)
    hrow = lambda h: slice(h * c, (h + 1) * c)
    s_bf = [s_ref[s, h].astype(BF16) for s, h in items]
    ws_qs = [_dot(jnp.concatenate([wu[s][hrow(h), :hw].astype(BF16),
                                   state["q"][chunk_rows[s], cols(h)]], axis=0), sb)
             for (s, h), sb in zip(items, s_bf)]
    d_bf = [(wu[s][hrow(h), hw:] - x[:c]).astype(BF16) for (s, h), x in zip(items, ws_qs)]
    upd = [_dot(jnp.concatenate([state["aqk"][s][:, hrow(h)], state["kd"][s][:, cols(h)].T], axis=0),
                d) for (s, h), d in zip(items, d_bf)]
    for (s, h), y in zip(items, upd):
        rs = chunk_rows[s]
        s_ref[s, h] = eg_all[rs.stop - 1:rs.stop, cols(h)] * s_ref[s, h] + y[c:]
    o = [eg_all[chunk_rows[s], cols(h)] * x[c:] + y[:c] for (s, h), x, y in zip(items, ws_qs, upd)]
    ms = [jnp.mean(x * x, axis=-1, keepdims=True) for x in o]
    for (s, h), x, m in zip(items, o, ms):
        out_ref[s, :, cols(h)] = (x * lax.rsqrt(m + EPS) * ng_ref[...]
                                  * szd_ref[s, :, cols(h)]).astype(out_ref.dtype)


def _gdn(qkv, conv_buf8, conv_w, gb, szd, s0, norm_g):
    bsz, length, _ = qkv.shape
    c, nh, hw = GDN_CHUNK, GDN_HEADS, GDN_DIM
    nseq = 4
    per = c // SUBLANE
    src = jnp.arange(LANE)[:, None]

    def spread(width):
        dst = jnp.arange(2 * nh * width)[None, :]
        return (src == dst // width).astype(BF16)

    e64, e128 = spread(c), spread(hw)
    blk = jnp.arange(nh * c) // c
    bd_mask = (blk[:, None] == blk[None, :]).astype(BF16)
    cur = lambda w: pl.BlockSpec((nseq, c, w), lambda b, i: (b, i, 0))
    prev = pl.BlockSpec((nseq, SUBLANE, CONV_CH), lambda b, i: (b, jnp.maximum(i * per - 1, 0), 0))
    buf = pl.BlockSpec((nseq, SUBLANE, CONV_CH), lambda b, i: (b, 0, 0))
    st = pl.BlockSpec((nseq, nh, hw, hw), lambda b, i: (b, 0, 0, 0))
    return pl.pallas_call(
        _gdn_body,
        grid=(bsz // nseq, length // c),
        in_specs=[cur(CONV_CH), prev, buf, cur(LANE), cur(MIX_W), st, _full(conv_w.shape),
                  _full(e64.shape), _full(e128.shape), _full(bd_mask.shape), _full((1, hw))],
        out_specs=[cur(MIX_W), st],
        out_shape=[jax.ShapeDtypeStruct((bsz, length, MIX_W), BF16),
                   jax.ShapeDtypeStruct(s0.shape, F32)],
        compiler_params=_cparams(("parallel", "arbitrary")),
        name="gdn",
    )(qkv, qkv, conv_buf8, gb, szd, s0, conv_w, e64, e128, bd_mask, norm_g)


def _row_tile(n):
    for tm in (512, 256, 128, 64):
        if n % tm == 0:
            return tm
    raise ValueError(n)


def _trunk(x, p, past, wts, s5w):
    bsz, length, _ = x.shape
    n = bsz * length
    first = past is None
    chunk = min(length, MLP_CHUNK)
    tm = min(length, 64)

    outs = _even_in(x, wts["norm_g0"], wts["even_w_in"], wts["a_ln_g"], wts["a_ln_b"],
                    wts["a_w_s"][:, :chunk, :chunk], wts["a_b_s"][:, :chunk, :],
                    chunk=chunk, tm=max(tm, chunk), want_av=not first)
    a_out, ub_tm, szb_tm = outs[:3]
    a_v = None if first else outs[3][None]
    if first:
        x0r = jnp.zeros((bsz, S5_LANES), F32)
        x0i = x0r
    else:
        x0r = past["b_re"].reshape(bsz, S5_LANES)
        x0i = past["b_im"].reshape(bsz, S5_LANES)
    b_out_tm, xr, xi = _s5(ub_tm, szb_tm, x0r, x0i, *s5w, wts["b_D"], wts["b_glu_w"], t_steps=tm)
    h = _mix_out(a_out, b_out_tm, x, p, 0, wts["even_w_out"], wts["ple_norm_g0"], wts["ple_gate_w0"],
                 wts["ple_proj0"], None, tm=tm, b_time_major=True)
    h = h.reshape(n, D_MODEL)

    tm1 = _row_tile(n)
    q, k, v, szc, qkv, szd, gb, k4, v4 = _odd_in(h, wts["norm_g1"], wts["odd_w_main"],
                                                 wts["odd_w_ab"], wts["d_A_log"], wts["d_dt_bias"],
                                                 tm=min(tm1, 256))
    to3 = lambda a: a.reshape(bsz, length, a.shape[-1])
    q, k, v, szc, qkv, szd, gb = map(to3, (q, k, v, szc, qkv, szd, gb))
    if first:
        c_out = _sb_attention(q, k, v, szc)
        buf8 = jnp.zeros((bsz, SUBLANE, CONV_CH), F32)
        s0 = jnp.zeros((bsz, GDN_HEADS, GDN_DIM, GDN_DIM), F32)
    else:
        c_out = _sb_attention(q, k, v, szc, past["k_c"], past["v_c"])
        buf8 = jnp.pad(past["conv_d"], ((0, 0), (SUBLANE - (CONV_W - 1), 0), (0, 0)))
        s0 = past["s_d"]
    d_out, s_new = _gdn(qkv, buf8, wts["d_conv_w"], gb, szd, s0, wts["d_norm_g"])
    y = _mix_out(c_out, d_out, h.reshape(bsz, length, D_MODEL), p, 1, wts["odd_w_out"],
                 wts["ple_norm_g1"], wts["ple_gate_w1"], wts["ple_proj1"], wts["final_norm_g"],
                 tm=tm, b_time_major=False)

    conv_new = qkv[:, length - (CONV_W - 1):, :]
    return (y,
            xr.reshape(1, bsz, S5_GROUPS, S5_N), xi.reshape(1, bsz, S5_GROUPS, S5_N), a_v,
            k4.reshape(1, bsz, length, SB_HEADS, SB_DIM), v4.reshape(1, bsz, length, SB_HEADS, SB_DIM),
            s_new[None], conv_new[None])


def kernel(x_prompt, x_sample, state_b_re, state_b_im, cache_k_c, cache_v_c, state_d, state_conv_d,
           p_prompt, p_sample,
           norm_g, final_norm_g, ple_proj, ple_gate_w, ple_norm_g,
           even_w_in, even_w_out, a_ln_g, a_ln_b, a_w_s, a_b_s,
           b_lam_re, b_lam_im, b_log_dt, b_B_re, b_B_im, b_C_re, b_C_im, b_D, b_glu_w,
           odd_w_in, odd_w_out, d_conv_w, d_A_log, d_dt_bias, d_norm_g):
    row = lambda a: a.reshape(1, -1).astype(F32)
    tril = jnp.tril(jnp.ones((MLP_CHUNK, MLP_CHUNK), dtype=bool))
    n_main = 8 * MIX_W
    pad_lane = lambda a: jnp.pad(a.reshape(1, -1).astype(F32), ((0, 0), (0, LANE - a.size)))
    wts = {
        "norm_g0": row(norm_g[0]), "norm_g1": row(norm_g[1]), "final_norm_g": row(final_norm_g),
        "ple_norm_g0": row(ple_norm_g[0]), "ple_norm_g1": row(ple_norm_g[1]),
        "ple_gate_w0": ple_gate_w[0].astype(BF16), "ple_gate_w1": ple_gate_w[1].astype(BF16),
        "ple_proj0": ple_proj[0].astype(BF16), "ple_proj1": ple_proj[1].astype(BF16),
        "even_w_in": even_w_in[0].astype(BF16), "even_w_out": even_w_out[0].astype(BF16),
        "a_ln_g": row(a_ln_g[0]), "a_ln_b": row(a_ln_b[0]),
        "a_w_s": jnp.where(tril, a_w_s[0], 0.0).astype(BF16),
        "a_b_s": jnp.broadcast_to(a_b_s[0][:, :, None], (A_GROUPS, MLP_CHUNK, LANE)).astype(F32),
        "b_D": row(b_D[0]), "b_glu_w": b_glu_w[0].astype(BF16),
        "odd_w_main": odd_w_in[0][:, :n_main].astype(BF16),
        "odd_w_ab": jnp.pad(odd_w_in[0][:, n_main:], ((0, 0), (0, LANE - 2 * GDN_HEADS))).astype(BF16),
        "odd_w_out": odd_w_out[0].astype(BF16),
        "d_conv_w": d_conv_w[0].astype(F32),
        "d_A_log": pad_lane(d_A_log[0]), "d_dt_bias": pad_lane(d_dt_bias[0]),
        "d_norm_g": row(d_norm_g[0]),
    }
    ar, ai, bbr, bbi = _s5_disc(b_lam_re[0], b_lam_im[0], b_log_dt[0], b_B_re[0], b_B_im[0])
    bre, bim, cre, cim = _s5_block_diag((bbr, bbi), b_C_re[0], b_C_im[0])
    s5w = (ar.reshape(1, S5_LANES), ai.reshape(1, S5_LANES), bre, bim, cre, cim)

    (y_p, b_re_p, b_im_p, _, k_p, v_p, s_p, conv_p) = _trunk(x_prompt, p_prompt, None, wts, s5w)
    past = {"b_re": state_b_re[0], "b_im": state_b_im[0],
            "k_c": jnp.transpose(cache_k_c[0], (0, 2, 3, 1)),
            "v_c": jnp.transpose(cache_v_c[0], (0, 2, 3, 1)),
            "s_d": state_d[0], "conv_d": state_conv_d[0]}
    (y_s, b_re_s, b_im_s, a_v_s, k_s, v_s, s_s, conv_s) = _trunk(x_sample, p_sample, past, wts, s5w)
    return (y_p, y_s, b_re_p, b_im_p, k_p, v_p, s_p, conv_p,
            b_re_s, b_im_s, a_v_s, k_s, v_s, s_s, conv_s)
```

```python
import functools
import math

import jax
import jax.numpy as jnp
from jax import lax
from jax.experimental import pallas as pl
from jax.experimental.pallas import tpu as pltpu

F32 = jnp.float32
BF16 = jnp.bfloat16
EPS = 1e-6

D_MODEL = 1024
PLE_DIM = 256
MIX_W = 512
A_GROUPS = 4
MLP_CHUNK = 128
S5_GROUPS = 32
S5_P = 16
S5_N = 64
S5_LANES = S5_GROUPS * S5_N
SB_HEADS = 8
SB_DIM = 64
SB_BLOCK = 128
GDN_HEADS = 4
GDN_DIM = 128
GDN_CHUNK = 64
CONV_W = 4
CONV_CH = 3 * MIX_W
LANE = 128
SUBLANE = 8
VMEM_LIMIT = 52 * 1024 * 1024
SB_DEAD = -104.0
SB_MASKED = -1e9


def _cparams(sem):
    return pltpu.CompilerParams(dimension_semantics=sem, vmem_limit_bytes=VMEM_LIMIT)


def _gelu(x):
    return 0.5 * x * (1.0 + jnp.tanh(0.7978845608028654 * (x + 0.044715 * (x * x * x))))


def _sigmoid(x):
    return 1.0 / (1.0 + jnp.exp(-x))


def _silu(x):
    return x * _sigmoid(x)


def _softplus(x):
    return jnp.maximum(x, 0.0) + jnp.log1p(jnp.exp(-jnp.abs(x)))


def _rms(x, g):
    ms = jnp.mean(x * x, axis=-1, keepdims=True)
    return x * lax.rsqrt(ms + EPS) * g


def _dot(a, b):
    return jnp.dot(a, b, preferred_element_type=F32)


def _dot_nt(a, b):
    return lax.dot_general(a, b, (((1,), (1,)), ((), ())), preferred_element_type=F32)


def _dot_tn(a, b):
    return lax.dot_general(a, b, (((0,), (0,)), ((), ())), preferred_element_type=F32)


def _split3(x):
    hi = x.astype(BF16)
    r = x - hi.astype(F32)
    mid = r.astype(BF16)
    lo = (r - mid.astype(F32)).astype(BF16)
    return hi, mid, lo


def _full(shape):
    n = len(shape)
    return pl.BlockSpec(shape, lambda *_: (0,) * n)


def _time_major(x, nseq):
    tm = x.shape[0] // nseq
    return pltpu.einshape("btd->tbd", x.reshape(nseq, tm, x.shape[1]))


def _even_in_body(x_ref, g_ref, w_ref, lng_ref, lnb_ref, ws_ref, bs_ref,
                  aout_ref, ub_ref, szb_ref, av_ref, *, chunk):
    nseq, tm, _ = x_ref.shape
    rows_all = nseq * tm
    hn = _rms(x_ref[...].reshape(rows_all, D_MODEL), g_ref[...]).astype(BF16)

    def proj(k):
        return _dot(hn, w_ref[:, k * MIX_W:(k + 1) * MIX_W])

    va = _gelu(proj(1))
    mu = jnp.mean(va, axis=-1, keepdims=True)
    vc = va - mu
    var = jnp.mean(vc * vc, axis=-1, keepdims=True)
    va = vc * lax.rsqrt(var + EPS) * lng_ref[...] + lnb_ref[...]
    if av_ref is not None:
        av_ref[...] = va.reshape(av_ref.shape)
    va_bf = va.astype(BF16)
    gate = _gelu(proj(0)) * _silu(proj(2))
    gw = MIX_W // A_GROUPS
    pieces = []
    for ch in range(rows_all // chunk):
        rows = slice(ch * chunk, (ch + 1) * chunk)
        s = [_dot(ws_ref[g], va_bf[rows, g * gw:(g + 1) * gw]) + bs_ref[g] for g in range(A_GROUPS)]
        pieces.append(gate[rows] * jnp.concatenate(s, axis=1))
    aout_ref[...] = jnp.concatenate(pieces, axis=0).astype(aout_ref.dtype).reshape(aout_ref.shape)
    ub_ref[...] = _time_major(proj(3), nseq)
    szb_ref[...] = _time_major(_silu(proj(4)), nseq)


def _even_in(x, norm_g, w_in, ln_g, ln_b, w_s, b_s, *, chunk, tm, want_av):
    bsz, length, _ = x.shape
    body = functools.partial(_even_in_body, chunk=chunk)
    if not want_av:
        body_fn = lambda *r: body(*r, None)
    else:
        body_fn = body
    seq = lambda w: pl.BlockSpec((SUBLANE, tm, w), lambda b, i: (b, i, 0))
    tmaj = pl.BlockSpec((tm, SUBLANE, MIX_W), lambda b, i: (i, b, 0))
    out_shape = [jax.ShapeDtypeStruct((bsz, length, MIX_W), BF16),
                 jax.ShapeDtypeStruct((length, bsz, MIX_W), F32),
                 jax.ShapeDtypeStruct((length, bsz, MIX_W), F32)]
    out_specs = [seq(MIX_W), tmaj, tmaj]
    if want_av:
        out_shape.append(jax.ShapeDtypeStruct((bsz, length, MIX_W), F32))
        out_specs.append(seq(MIX_W))
    return pl.pallas_call(
        body_fn,
        grid=(bsz // SUBLANE, length // tm),
        in_specs=[seq(D_MODEL),
                  _full((1, D_MODEL)),
                  _full(w_in.shape),
                  _full((1, MIX_W)), _full((1, MIX_W)),
                  _full(w_s.shape), _full(b_s.shape)],
        out_specs=out_specs,
        out_shape=out_shape,
        compiler_params=_cparams(("parallel", "parallel")),
        name="even_in",
    )(x, norm_g, w_in, ln_g, ln_b, w_s, b_s)


def _s5_disc_body(lr_ref, li_ref, ldt_ref, br_ref, bi_ref, ar_ref, ai_ref, bbr_ref, bbi_ref):
    dt = jnp.exp(ldt_ref[...])
    lr, li = lr_ref[...], li_ref[...]
    mag = jnp.exp(lr * dt)
    ar = mag * jnp.cos(li * dt)
    ai = mag * jnp.sin(li * dt)
    den = lr * lr + li * li
    cr = ((ar - 1.0) * lr + ai * li) / den
    ci = (ai * lr - (ar - 1.0) * li) / den
    ar_ref[...] = ar
    ai_ref[...] = ai
    for p in range(S5_P):
        br, bi = br_ref[p], bi_ref[p]
        bbr_ref[p] = cr * br - ci * bi
        bbi_ref[p] = cr * bi + ci * br


def _s5_disc(lam_re, lam_im, log_dt, b_re, b_im):
    gn = (S5_GROUPS, S5_N)
    pgn = (S5_P, S5_GROUPS, S5_N)
    return pl.pallas_call(
        _s5_disc_body,
        out_shape=[jax.ShapeDtypeStruct(gn, F32), jax.ShapeDtypeStruct(gn, F32),
                   jax.ShapeDtypeStruct(pgn, F32), jax.ShapeDtypeStruct(pgn, F32)],
        name="s5_disc",
    )(lam_re, lam_im, log_dt.reshape(S5_GROUPS, 1),
      jnp.transpose(b_re, (2, 0, 1)), jnp.transpose(b_im, (2, 0, 1)))


def _s5_block_diag(bb_pgn, c_re, c_im):
    gpc = LANE // S5_P
    nj = S5_GROUPS // gpc
    eye = jnp.eye(gpc, dtype=F32)

    def b_blk(bb):
        b = jnp.transpose(bb, (1, 0, 2)).reshape(nj, gpc, S5_P, S5_N)
        m = b[:, :, :, None, :] * eye[None, :, None, :, None]
        return m.reshape(nj, gpc * S5_P, gpc * S5_N).astype(BF16)

    def c_blk(c):
        cc = jnp.transpose(c, (0, 2, 1)).reshape(nj, gpc, S5_N, S5_P)
        m = cc[:, :, :, None, :] * eye[None, :, None, :, None]
        return m.reshape(nj, gpc * S5_N, gpc * S5_P).astype(BF16)

    return b_blk(bb_pgn[0]), b_blk(bb_pgn[1]), c_blk(c_re), c_blk(-c_im)


def _s5_body(u_ref, szb_ref, x0r_ref, x0i_ref, ar_ref, ai_ref, bre_ref, bim_ref, cre_ref, cim_ref,
             d_ref, glu_ref, out_ref, st_r, st_i, sre, sim):
    t_steps = u_ref.shape[0]
    rows = t_steps * SUBLANE
    ti = pl.program_id(1)
    cw = bre_ref.shape[2]
    nj = bre_ref.shape[0]

    @pl.when(ti == 0)
    def _():
        st_r[...] = x0r_ref[...]
        st_i[...] = x0i_ref[...]

    u = u_ref[...].reshape(rows, MIX_W)
    u_bf = u.astype(BF16)
    for j in range(nj):
        uj = u_bf[:, j * LANE:(j + 1) * LANE]
        sre[:, j * cw:(j + 1) * cw] = _dot(uj, bre_ref[j])
        sim[:, j * cw:(j + 1) * cw] = _dot(uj, bim_ref[j])

    for j in range(0, nj, 2):
        lanes = slice(j * cw, (j + 2) * cw)
        a_r = jnp.broadcast_to(ar_ref[:, lanes], (SUBLANE, 2 * cw))
        a_i = jnp.broadcast_to(ai_ref[:, lanes], (SUBLANE, 2 * cw))

        def step(t, carry, lanes=lanes, a_r=a_r, a_i=a_i):
            xr, xi = carry
            r0 = pl.multiple_of(t * SUBLANE, SUBLANE)
            nxr = a_r * xr - a_i * xi + sre[pl.ds(r0, SUBLANE), lanes]
            nxi = a_r * xi + a_i * xr + sim[pl.ds(r0, SUBLANE), lanes]
            sre[pl.ds(r0, SUBLANE), lanes] = nxr
            sim[pl.ds(r0, SUBLANE), lanes] = nxi
            return nxr, nxi

        xr, xi = lax.fori_loop(0, t_steps, step, (st_r[:, lanes], st_i[:, lanes]), unroll=8)
        st_r[:, lanes] = xr
        st_i[:, lanes] = xi

    ys = []
    for j in range(nj):
        lanes = slice(j * cw, (j + 1) * cw)
        ys.append(_dot(sre[:, lanes].astype(BF16), cre_ref[j])
                  + _dot(sim[:, lanes].astype(BF16), cim_ref[j]))
    y = jnp.concatenate(ys, axis=1) + d_ref[...] * u
    y = _gelu(y)
    y = y * _sigmoid(_dot(y.astype(BF16), glu_ref[...]))
    y = y * szb_ref[...].reshape(rows, MIX_W)
    out_ref[...] = y.reshape(out_ref.shape)


def _s5(u_tm, szb_tm, x0r, x0i, ar, ai, bre, bim, cre, cim, d_skip, glu_w, *, t_steps):
    length, nb, _ = u_tm.shape
    blk = pl.BlockSpec((t_steps, SUBLANE, MIX_W), lambda b, t: (t, b, 0))
    st = pl.BlockSpec((SUBLANE, S5_LANES), lambda b, t: (b, 0))
    rows = t_steps * SUBLANE
    return pl.pallas_call(
        _s5_body,
        grid=(nb // SUBLANE, length // t_steps),
        in_specs=[blk, blk, st, st, _full((1, S5_LANES)), _full((1, S5_LANES)),
                  _full(bre.shape), _full(bim.shape), _full(cre.shape), _full(cim.shape),
                  _full((1, MIX_W)), _full(glu_w.shape)],
        out_specs=[blk, st, st],
        out_shape=[jax.ShapeDtypeStruct((length, nb, MIX_W), F32),
                   jax.ShapeDtypeStruct((nb, S5_LANES), F32),
                   jax.ShapeDtypeStruct((nb, S5_LANES), F32)],
        scratch_shapes=[pltpu.VMEM((rows, S5_LANES), F32), pltpu.VMEM((rows, S5_LANES), F32)],
        compiler_params=_cparams(("parallel", "arbitrary")),
        name="s5_scan",
    )(u_tm, szb_tm, x0r, x0i, ar, ai, bre, bim, cre, cim, d_skip, glu_w)


def _mix_out_body(a_ref, b_ref, h_ref, p_ref, wout_ref, pg_ref, wgate_ref, wp_ref, fg_ref, out_ref,
                  *, b_time_major):
    rows = out_ref.shape[0] * out_ref.shape[1]
    flat = lambda ref: ref[...].reshape(rows, ref.shape[-1])
    if b_time_major:
        b = jnp.concatenate([b_ref[:, s, :] for s in range(b_ref.shape[1])], axis=0)
    else:
        b = flat(b_ref)
    mix = _dot(flat(a_ref), wout_ref[:MIX_W, :]) + _dot(b.astype(BF16), wout_ref[MIX_W:, :])
    h1 = flat(h_ref) + mix
    gate = _sigmoid(_dot(_rms(h1, pg_ref[...]).astype(BF16), wgate_ref[...]))
    h2 = h1 + gate * _dot(flat(p_ref).astype(BF16), wp_ref[...])
    if fg_ref is not None:
        h2 = _rms(h2, fg_ref[...])
    out_ref[...] = h2.reshape(out_ref.shape)


def _mix_out(a, b, h, p, layer, w_out, ple_g, w_gate, w_p, final_g, *, tm, b_time_major):
    bsz, length, _ = h.shape
    seq = lambda w: pl.BlockSpec((SUBLANE, tm, w), lambda s, i: (s, i, 0))
    b_spec = pl.BlockSpec((tm, SUBLANE, MIX_W), lambda s, i: (i, s, 0)) if b_time_major else seq(MIX_W)
    args = [a, b, h, p, w_out, ple_g, w_gate, w_p]
    in_specs = [seq(MIX_W), b_spec, seq(D_MODEL),
                pl.BlockSpec((None, SUBLANE, tm, PLE_DIM), lambda s, i: (layer, s, i, 0)),
                _full(w_out.shape), _full((1, D_MODEL)), _full(w_gate.shape), _full(w_p.shape)]
    body = functools.partial(_mix_out_body, b_time_major=b_time_major)
    if final_g is None:
        body_fn = lambda *r: body(*r[:8], None, r[8])
    else:
        body_fn = body
        args.append(final_g)
        in_specs.append(_full((1, D_MODEL)))
    return pl.pallas_call(
        body_fn,
        grid=(bsz // SUBLANE, length // tm),
        in_specs=in_specs,
        out_specs=seq(D_MODEL),
        out_shape=jax.ShapeDtypeStruct((bsz, length, D_MODEL), F32),
        compiler_params=_cparams(("parallel", "parallel")),
        name="mix_out",
    )(*args)


def _odd_in_body(h_ref, g_ref, w_ref, wab_ref, alog_ref, dtb_ref,
                 q_ref, k_ref, v_ref, szc_ref, qkv_ref, szd_ref, gb_ref, k4_ref, v4_ref):
    hn = _rms(h_ref[...], g_ref[...]).astype(BF16)

    def proj(k, width=MIX_W):
        return _dot(hn, w_ref[:, k * MIX_W:k * MIX_W + width])

    def head_major(x):
        return pltpu.einshape("t(hd)->thd", x, h=SB_HEADS)

    q_ref[...] = (proj(0) * (SB_DIM ** -0.5)).astype(BF16)
    kc, vc = proj(1), proj(2)
    k_ref[...] = kc.astype(BF16)
    v_ref[...] = vc.astype(BF16)
    k4_ref[...] = head_major(kc)
    v4_ref[...] = head_major(vc)
    szc_ref[...] = _silu(proj(3))
    for k in range(3):
        qkv_ref[:, k * MIX_W:(k + 1) * MIX_W] = proj(4 + k)
    szd_ref[...] = _silu(proj(7))
    ab = _dot(hn, wab_ref[...])
    g = -jnp.exp(alog_ref[...]) * _softplus(ab + dtb_ref[...])
    lane = lax.broadcasted_iota(jnp.int32, ab.shape, 1)
    gb_ref[...] = jnp.where(lane < GDN_HEADS, g, _sigmoid(ab))


def _odd_in(h2d, norm_g, w_main, w_ab, a_log, dt_bias, *, tm):
    n = h2d.shape[0]
    row = lambda i: (i, 0)
    widths = [MIX_W, MIX_W, MIX_W, MIX_W, CONV_CH, MIX_W, LANE]
    dtypes = [BF16, BF16, BF16, F32, F32, F32, F32]
    head4 = pl.BlockSpec((tm, SB_HEADS, SB_DIM), lambda i: (i, 0, 0))
    head4_shape = jax.ShapeDtypeStruct((n, SB_HEADS, SB_DIM), F32)
    return pl.pallas_call(
        _odd_in_body,
        grid=(n // tm,),
        in_specs=[pl.BlockSpec((tm, D_MODEL), row), _full((1, D_MODEL)),
                  _full(w_main.shape), _full(w_ab.shape), _full((1, LANE)), _full((1, LANE))],
        out_specs=[pl.BlockSpec((tm, w), row) for w in widths] + [head4, head4],
        out_shape=[jax.ShapeDtypeStruct((n, w), d) for w, d in zip(widths, dtypes)]
        + [head4_shape, head4_shape],
        compiler_params=_cparams(("parallel",)),
        name="odd_in",
    )(h2d, norm_g, w_main, w_ab, a_log, dt_bias)


def _sb_pairs(q2x, k_blocks, v_blocks, masks, valids, c_in, cum_rhs, key_major):
    npair = len(q2x)
    nblk = len(k_blocks[0])
    rows = q2x[0].shape[0]
    if key_major:
        z = [_dot(q2x[p], jnp.concatenate(k_blocks[p], axis=1)) for p in range(npair)]
    else:
        z = [_dot_nt(q2x[p], jnp.concatenate(k_blocks[p], axis=0)) for p in range(npair)]
    log_beta, hilo = [], []
    for p in range(npair):
        lb, hl = [], []
        for j in range(nblk):
            zj = z[p][:, j * LANE:(j + 1) * LANE]
            if masks[j] is not None:
                zj = jnp.where(masks[j], zj, SB_MASKED)
            if valids[p][j] is not None:
                zj = jnp.where(valids[p][j], zj, SB_MASKED)
            l1p = jnp.log(1.0 + jnp.exp(-jnp.abs(zj)))
            lb.append(jnp.minimum(zj, 0.0) - l1p)
            log_keep = lb[-1] - zj
            hi = log_keep.astype(BF16)
            lo = (log_keep - hi.astype(F32)).astype(BF16)
            hl.append(jnp.concatenate([hi, lo], axis=1))
        log_beta.append(lb)
        hilo.append(jnp.concatenate(hl, axis=0))
    cum = [_dot(x, cum_rhs) for x in hilo]
    ws, totals = [], []
    for p in range(npair):
        c = c_in[p]
        w_p = [None] * nblk
        for j in reversed(range(nblk)):
            after = cum[p][j * rows:(j + 1) * rows, :LANE]
            if c is not None:
                after = after + c
            w_p[j] = jnp.exp(log_beta[p][j] + after).astype(BF16)
            total = cum[p][j * rows:(j + 1) * rows, LANE:]
            c = total if c is None else c + total
        ws.append(jnp.concatenate(w_p, axis=1))
        totals.append(c)
    if key_major:
        pv = [_dot_nt(ws[p], jnp.concatenate(v_blocks[p], axis=1)) for p in range(npair)]
    else:
        pv = [_dot(ws[p], jnp.concatenate(v_blocks[p], axis=0)) for p in range(npair)]
    return pv, totals


def _sb_body(q_ref, kd_ref, vd_ref, k1_ref, v1_ref, k2_ref, v2_ref, szc_ref, cum_ref,
             kold_ref, vold_ref, out_ref, acc_ref, c_ref, kbuf, vbuf, sem, *, top_of, key_major):
    b = pl.program_id(0)
    i = pl.program_id(1)
    tq = q_ref.shape[0]
    tk = kbuf.shape[-1] if key_major else kbuf.shape[0]
    npair = MIX_W // LANE
    pairs = range(npair)
    tt = lax.broadcasted_iota(jnp.int32, (2 * tq, tk), 0)
    ss = lax.broadcasted_iota(jnp.int32, (2 * tq, tk), 1)
    diag_mask = ss < jnp.where(tt >= tq, tt - tq, tt)
    even_lane = lax.broadcasted_iota(jnp.int32, (tq, LANE), 1) < SB_DIM
    top = top_of(i)

    def diag_tile(ref, p):
        x = ref[:, p * LANE:(p + 1) * LANE]
        if key_major:
            x = x.astype(F32).T
            if tq < tk:
                x = jnp.concatenate([x, jnp.zeros((LANE, tk - tq), F32)], axis=1)
        elif tq < tk:
            x = jnp.concatenate([x, jnp.zeros((tk - tq, LANE), x.dtype)], axis=0)
        return x.astype(BF16)

    def old_tile(ref, p):
        if key_major:
            return jnp.concatenate([ref[2 * p], ref[2 * p + 1]], axis=0).astype(BF16)
        return ref[:, p * LANE:(p + 1) * LANE].astype(BF16)

    def q_pair(p):
        q2 = q_ref[:, p * LANE:(p + 1) * LANE]
        zero = jnp.zeros_like(q2)
        return jnp.concatenate([jnp.where(even_lane, q2, zero), jnp.where(even_lane, zero, q2)], axis=0)

    def merge(pv):
        return jnp.where(even_lane, pv[:tq], pv[tq:])

    q2x = [q_pair(p) for p in pairs]
    k_win = [[old_tile(k2_ref, p), old_tile(k1_ref, p), diag_tile(kd_ref, p)] for p in pairs]
    v_win = [[old_tile(v2_ref, p), old_tile(v1_ref, p), diag_tile(vd_ref, p)] for p in pairs]
    masks = [None, None, diag_mask]
    valids = [[top >= 1, top >= 0, None]] * npair
    cum_rhs = cum_ref[...]
    pvs, sums = _sb_pairs(q2x, k_win, v_win, masks, valids, [None] * npair, cum_rhs, key_major)
    acc_all = jnp.concatenate([merge(pv) for pv in pvs], axis=1)
    out_ref[...] = (acc_all * szc_ref[...]).astype(out_ref.dtype)
    c_max = functools.reduce(jnp.maximum, sums)

    @pl.when(jnp.logical_and(top >= 2, jnp.max(c_max) > SB_DEAD))
    def _():
        acc_ref[...] = acc_all
        for p in pairs:
            c_ref[p] = sums[p]

        def cond(carry):
            j, go = carry
            return jnp.logical_and(j >= 0, go)

        def body(carry):
            j, _ = carry
            r0 = pl.multiple_of(j * tk, tk)
            if key_major:
                src = lambda ref: ref.at[b, :, :, pl.ds(r0, tk)]
            else:
                src = lambda ref: ref.at[b, pl.ds(r0, tk)]
            ck = pltpu.make_async_copy(src(kold_ref), kbuf, sem.at[0])
            cv = pltpu.make_async_copy(src(vold_ref), vbuf, sem.at[1])
            ck.start()
            cv.start()
            ck.wait()
            cv.wait()
            pvs, cs = _sb_pairs(q2x, [[old_tile(kbuf, p)] for p in pairs],
                                [[old_tile(vbuf, p)] for p in pairs], [None], [[None]] * npair,
                                [c_ref[p] for p in pairs], cum_rhs, key_major)
            for p in pairs:
                acc_ref[:, p * LANE:(p + 1) * LANE] += merge(pvs[p])
                c_ref[p] = cs[p]
            return j - 1, jnp.max(c_ref[...]) > SB_DEAD

        lax.while_loop(cond, body, (top - 2, jnp.max(c_max) > SB_DEAD))
        out_ref[...] = (acc_ref[...] * szc_ref[...]).astype(out_ref.dtype)


def _sb_cum_rhs(tk):
    jj = lax.broadcasted_iota(jnp.int32, (tk, tk), 0)
    ss = lax.broadcasted_iota(jnp.int32, (tk, tk), 1)
    half = jnp.concatenate([(jj > ss).astype(BF16), jnp.ones((tk, LANE), BF16)], axis=1)
    return jnp.concatenate([half, half], axis=0)


def _sb_attention(q, k_new, v_new, szc, k_past=None, v_past=None):
    bsz, length, _ = q.shape
    tq = min(length, SB_BLOCK)
    nq = length // tq
    tk = SB_BLOCK
    key_major = k_past is not None
    if not key_major:
        assert tq == tk
        k_old, v_old = k_new, v_new
        top_of = lambda i: i - 1
        old_blk = (tk, MIX_W)
        old_idx = lambda b, blk: (b, blk, 0)
    else:
        assert nq == 1 and k_past.shape[-1] % tk == 0
        k_old, v_old = k_past, v_past
        n_old = k_past.shape[-1] // tk
        top_of = lambda i: i * 0 + (n_old - 1)
        old_blk = (SB_HEADS, SB_DIM, tk)
        old_idx = lambda b, blk: (b, 0, 0, blk)
    cur = pl.BlockSpec((None, tq, MIX_W), lambda b, i: (b, i, 0))
    old1 = pl.BlockSpec((None,) + old_blk, lambda b, i: old_idx(b, jnp.maximum(top_of(i), 0)))
    old2 = pl.BlockSpec((None,) + old_blk, lambda b, i: old_idx(b, jnp.maximum(top_of(i) - 1, 0)))
    anyspec = pl.BlockSpec(memory_space=pl.ANY)
    cum_rhs = _sb_cum_rhs(tk)
    return pl.pallas_call(
        functools.partial(_sb_body, top_of=top_of, key_major=key_major),
        grid=(bsz, nq),
        in_specs=[cur, cur, cur, old1, old1, old2, old2, cur, _full(cum_rhs.shape),
                  anyspec, anyspec],
        out_specs=cur,
        out_shape=jax.ShapeDtypeStruct((bsz, length, MIX_W), BF16),
        scratch_shapes=[pltpu.VMEM((tq, MIX_W), F32),
                        pltpu.VMEM((MIX_W // LANE, 2 * tq, LANE), F32),
                        pltpu.VMEM(old_blk, k_old.dtype), pltpu.VMEM(old_blk, v_old.dtype),
                        pltpu.SemaphoreType.DMA((2,))],
        compiler_params=_cparams(("parallel", "parallel")),
        name="sb_attention",
    )(q, k_new, v_new, k_old, v_old, k_old, v_old, szc, cum_rhs, k_old, v_old)


def _dot01(m01, x):
    hi, mid, lo = _split3(x)
    return _dot(m01, hi) + _dot(m01, mid) + _dot(m01, lo)


def _dot01_r(x, m01):
    hi, mid, lo = _split3(x)
    return _dot(hi, m01) + _dot(mid, m01) + _dot(lo, m01)


def _gdn_body(cur_ref, prev_ref, buf_ref, gb_ref, szd_ref, s0_ref, cw_ref, e64_ref, e128_ref,
              bd_ref, ng_ref, out_ref, s_ref):
    c = GDN_CHUNK
    nseq = cur_ref.shape[0]
    rows = nseq * c
    nh, hw = GDN_HEADS, GDN_DIM
    hc = nh * c
    sh = int(math.log2(c))
    first = pl.program_id(1) == 0
    chunk_rows = [slice(s * c, (s + 1) * c) for s in range(nseq)]

    @pl.when(first)
    def _():
        s_ref[...] = s0_ref[...]

    def conv_silu(part):
        cols = slice(part * MIX_W, (part + 1) * MIX_W)
        cur = cur_ref[:, :, cols].reshape(rows, MIX_W)
        cw = cw_ref[:, cols]

        def conv(shifted):
            acc = shifted(CONV_W - 1) * cw[0:1, :]
            for t in range(1, CONV_W):
                acc = acc + shifted(CONV_W - 1 - t) * cw[t:t + 1, :]
            return acc

        y = conv(lambda d: pltpu.roll(cur, d, axis=0) if d else cur)
        pieces = []
        for s, rs in enumerate(chunk_rows):
            prev = jnp.where(first, buf_ref[s, :, cols], prev_ref[s, :, cols])
            head = jnp.concatenate([prev, cur[rs.start:rs.start + SUBLANE]], axis=0)
            pieces.append(conv(lambda d, head=head: head[SUBLANE - d:2 * SUBLANE - d]))
            pieces.append(y[rs.start + SUBLANE:rs.stop])
        return _silu(jnp.concatenate(pieces, axis=0))

    def l2n(x, scale):
        parts = []
        for h in range(nh):
            xh = x[:, h * hw:(h + 1) * hw]
            parts.append(xh * (lax.rsqrt(jnp.sum(xh * xh, axis=-1, keepdims=True) + EPS) * scale))
        return jnp.concatenate(parts, axis=1)

    k = l2n(conv_silu(1), 1.0)
    k_bf = k.astype(BF16)

    gb = gb_ref[...].reshape(rows, LANE)
    r_i = lax.broadcasted_iota(jnp.int32, (rows, rows), 0)
    c_i = lax.broadcasted_iota(jnp.int32, (rows, rows), 1)
    same_chunk = lax.shift_right_logical(r_i, sh) == lax.shift_right_logical(c_i, sh)
    g_all = _dot01(jnp.logical_and(same_chunk, c_i <= r_i).astype(BF16), gb)
    lane = lax.broadcasted_iota(jnp.int32, gb.shape, 1)
    x = jnp.where(lane < nh, g_all, gb)
    s64 = _dot01_r(x, e64_ref[...])
    s128 = _dot01_r(x, e128_ref[...])
    eg_all = jnp.exp(s128[:, :MIX_W])

    t_i = lax.broadcasted_iota(jnp.int32, (c, hc), 0)
    s_i = jnp.bitwise_and(lax.broadcasted_iota(jnp.int32, (c, hc), 1), c - 1)
    causal = s_i <= t_i
    strict = s_i < t_i
    eye = s_i == t_i
    blk_of_lane = lax.shift_right_logical(lax.broadcasted_iota(jnp.int32, (c, hc), 1), sh)
    bd_mask = bd_ref[...]

    def block_diag(x_bf):
        return jnp.concatenate([x_bf] * nh, axis=0) * bd_mask

    def diag_blocks(p):
        out = None
        for h in range(nh):
            sel = jnp.where(blk_of_lane == h, p[h * c:(h + 1) * c, :], 0.0)
            out = sel if out is None else out + sel
        return out

    def heads_on_rows(x_bf):
        return jnp.concatenate([x_bf[:, h * hw:(h + 1) * hw] for h in range(nh)], axis=0)

    tinv, pw, decay, k_rows = [], [], [], []
    for rs in chunk_rows:
        g64, b64 = s64[rs, :hc], s64[rs, hc:]
        g_row = jnp.sum(jnp.where(eye, g64, 0.0), axis=0, keepdims=True)
        decay.append(jnp.exp(jnp.where(causal, g64 - g_row, -1e30)))
        k_rows.append(heads_on_rows(k_bf[rs]))
        nmat = jnp.where(strict, b64 * decay[-1] * diag_blocks(_dot_nt(k_rows[-1], k_rows[-1])), 0.0)
        tinv.append(jnp.where(eye, 1.0, 0.0) - nmat)
        pw.append(nmat.astype(BF16))

    state = {}

    def fill_q():
        state["q"] = l2n(conv_silu(0), hw ** -0.5).astype(BF16)

    def fill_qk():
        state["aqk"] = []
        for rs, kr, dc in zip(chunk_rows, k_rows, decay):
            qk = diag_blocks(_dot_nt(heads_on_rows(state["q"][rs]), kr))
            state["aqk"].append(jnp.where(causal, qk * dc, 0.0).astype(BF16))

    def fill_v():
        state["v"] = conv_silu(2)

    def fill_rhs():
        state["rhs"] = []
        for rs in chunk_rows:
            b128 = s128[rs, MIX_W:]
            bk = (b128 * eg_all[rs] * k[rs]).astype(BF16)
            bv = (b128 * state["v"][rs]).astype(BF16)
            state["rhs"].append(jnp.concatenate([heads_on_rows(bk), heads_on_rows(bv)], axis=1))

    def fill_kd():
        state["kd"] = []
        for rs in chunk_rows:
            g128 = s128[rs, :MIX_W]
            state["kd"].append((k[rs] * jnp.exp(g128[c - 1:c, :] - g128)).astype(BF16))

    fillers = [fill_q, fill_qk, fill_v, fill_rhs, fill_kd]
    assert len(fillers) == sh - 1
    for level in range(sh - 1):
        pw = [_dot(p, block_diag(p)).astype(BF16) for p in pw]
        fillers[level]()
        tinv = [t + _dot(t.astype(BF16), block_diag(p)) for t, p in zip(tinv, pw)]
    wu = [_dot(block_diag(t.astype(BF16)), rhs) for t, rhs in zip(tinv, state["rhs"])]

    items = [(s, h) for s in range(nseq) for h in range(nh)]
    cols = lambda h: slice(h * hw, (h + 1) * hw)
    hrow = lambda h: slice(h * c, (h + 1) * c)
    s_bf = [s_ref[s, h].astype(BF16) for s, h in items]
    ws_qs = [_dot(jnp.concatenate([wu[s][hrow(h), :hw].astype(BF16),
                                   state["q"][chunk_rows[s], cols(h)]], axis=0), sb)
             for (s, h), sb in zip(items, s_bf)]
    d_bf = [(wu[s][hrow(h), hw:] - x[:c]).astype(BF16) for (s, h), x in zip(items, ws_qs)]
    upd = [_dot(jnp.concatenate([state["aqk"][s][:, hrow(h)], state["kd"][s][:, cols(h)].T], axis=0),
                d) for (s, h), d in zip(items, d_bf)]
    for (s, h), y in zip(items, upd):
        rs = chunk_rows[s]
        s_ref[s, h] = eg_all[rs.stop - 1:rs.stop, cols(h)] * s_ref[s, h] + y[c:]
    o = [eg_all[chunk_rows[s], cols(h)] * x[c:] + y[:c] for (s, h), x, y in zip(items, ws_qs, upd)]
    ms = [jnp.mean(x * x, axis=-1, keepdims=True) for x in o]
    for (s, h), x, m in zip(items, o, ms):
        out_ref[s, :, cols(h)] = (x * lax.rsqrt(m + EPS) * ng_ref[...]
                                  * szd_ref[s, :, cols(h)]).astype(out_ref.dtype)


def _gdn(qkv, conv_buf8, conv_w, gb, szd, s0, norm_g):
    bsz, length, _ = qkv.shape
    c, nh, hw = GDN_CHUNK, GDN_HEADS, GDN_DIM
    nseq = GDN_SEQS
    per = c // SUBLANE
    src = jnp.arange(LANE)[:, None]

    def spread(width):
        dst = jnp.arange(2 * nh * width)[None, :]
        return (src == dst // width).astype(BF16)

    e64, e128 = spread(c), spread(hw)
    blk = jnp.arange(nh * c) // c
    bd_mask = (blk[:, None] == blk[None, :]).astype(BF16)
    cur = lambda w: pl.BlockSpec((nseq, c, w), lambda b, i: (b, i, 0))
    prev = pl.BlockSpec((nseq, SUBLANE, CONV_CH), lambda b, i: (b, jnp.maximum(i * per - 1, 0), 0))
    buf = pl.BlockSpec((nseq, SUBLANE, CONV_CH), lambda b, i: (b, 0, 0))
    st = pl.BlockSpec((nseq, nh, hw, hw), lambda b, i: (b, 0, 0, 0))
    return pl.pallas_call(
        _gdn_body,
        grid=(bsz // nseq, length // c),
        in_specs=[cur(CONV_CH), prev, buf, cur(LANE), cur(MIX_W), st, _full(conv_w.shape),
                  _full(e64.shape), _full(e128.shape), _full(bd_mask.shape), _full((1, hw))],
        out_specs=[cur(MIX_W), st],
        out_shape=[jax.ShapeDtypeStruct((bsz, length, MIX_W), BF16),
                   jax.ShapeDtypeStruct(s0.shape, F32)],
        compiler_params=_cparams(("parallel", "arbitrary")),
        name="gdn",
    )(qkv, qkv, conv_buf8, gb, szd, s0, conv_w, e64, e128, bd_mask, norm_g)


TIME_TILE = 128
ROW_TILE = 512
GDN_SEQS = 8


def _trunk(x, p, past, wts, s5w):
    bsz, length, _ = x.shape
    n = bsz * length
    first = past is None
    chunk = min(length, MLP_CHUNK)
    tm = min(length, TIME_TILE)

    outs = _even_in(x, wts["norm_g0"], wts["even_w_in"], wts["a_ln_g"], wts["a_ln_b"],
                    wts["a_w_s"][:, :chunk, :chunk], wts["a_b_s"][:, :chunk, :],
                    chunk=chunk, tm=tm, want_av=not first)
    a_out, ub_tm, szb_tm = outs[:3]
    a_v = None if first else outs[3][None]
    if first:
        x0r = jnp.zeros((bsz, S5_LANES), F32)
        x0i = x0r
    else:
        x0r = past["b_re"].reshape(bsz, S5_LANES)
        x0i = past["b_im"].reshape(bsz, S5_LANES)
    b_out_tm, xr, xi = _s5(ub_tm, szb_tm, x0r, x0i, *s5w, wts["b_D"], wts["b_glu_w"], t_steps=tm)
    h = _mix_out(a_out, b_out_tm, x, p, 0, wts["even_w_out"], wts["ple_norm_g0"], wts["ple_gate_w0"],
                 wts["ple_proj0"], None, tm=tm, b_time_major=True)
    h = h.reshape(n, D_MODEL)

    q, k, v, szc, qkv, szd, gb, k4, v4 = _odd_in(h, wts["norm_g1"], wts["odd_w_main"],
                                                 wts["odd_w_ab"], wts["d_A_log"], wts["d_dt_bias"],
                                                 tm=min(n, ROW_TILE))
    to3 = lambda a: a.reshape(bsz, length, a.shape[-1])
    q, k, v, szc, qkv, szd, gb = map(to3, (q, k, v, szc, qkv, szd, gb))
    if first:
        c_out = _sb_attention(q, k, v, szc)
        buf8 = jnp.zeros((bsz, SUBLANE, CONV_CH), F32)
        s0 = jnp.zeros((bsz, GDN_HEADS, GDN_DIM, GDN_DIM), F32)
    else:
        c_out = _sb_attention(q, k, v, szc, past["k_c"], past["v_c"])
        buf8 = jnp.pad(past["conv_d"], ((0, 0), (SUBLANE - (CONV_W - 1), 0), (0, 0)))
        s0 = past["s_d"]
    d_out, s_new = _gdn(qkv, buf8, wts["d_conv_w"], gb, szd, s0, wts["d_norm_g"])
    y = _mix_out(c_out, d_out, h.reshape(bsz, length, D_MODEL), p, 1, wts["odd_w_out"],
                 wts["ple_norm_g1"], wts["ple_gate_w1"], wts["ple_proj1"], wts["final_norm_g"],
                 tm=tm, b_time_major=False)

    conv_new = qkv[:, length - (CONV_W - 1):, :]
    return (y,
            xr.reshape(1, bsz, S5_GROUPS, S5_N), xi.reshape(1, bsz, S5_GROUPS, S5_N), a_v,
            k4.reshape(1, bsz, length, SB_HEADS, SB_DIM), v4.reshape(1, bsz, length, SB_HEADS, SB_DIM),
            s_new[None], conv_new[None])


def kernel(x_prompt, x_sample, state_b_re, state_b_im, cache_k_c, cache_v_c, state_d, state_conv_d,
           p_prompt, p_sample,
           norm_g, final_norm_g, ple_proj, ple_gate_w, ple_norm_g,
           even_w_in, even_w_out, a_ln_g, a_ln_b, a_w_s, a_b_s,
           b_lam_re, b_lam_im, b_log_dt, b_B_re, b_B_im, b_C_re, b_C_im, b_D, b_glu_w,
           odd_w_in, odd_w_out, d_conv_w, d_A_log, d_dt_bias, d_norm_g):
    row = lambda a: a.reshape(1, -1).astype(F32)
    tril = jnp.tril(jnp.ones((MLP_CHUNK, MLP_CHUNK), dtype=bool))
    n_main = 8 * MIX_W
    pad_lane = lambda a: jnp.pad(a.reshape(1, -1).astype(F32), ((0, 0), (0, LANE - a.size)))
    wts = {
        "norm_g0": row(norm_g[0]), "norm_g1": row(norm_g[1]), "final_norm_g": row(final_norm_g),
        "ple_norm_g0": row(ple_norm_g[0]), "ple_norm_g1": row(ple_norm_g[1]),
        "ple_gate_w0": ple_gate_w[0].astype(BF16), "ple_gate_w1": ple_gate_w[1].astype(BF16),
        "ple_proj0": ple_proj[0].astype(BF16), "ple_proj1": ple_proj[1].astype(BF16),
        "even_w_in": even_w_in[0].astype(BF16), "even_w_out": even_w_out[0].astype(BF16),
        "a_ln_g": row(a_ln_g[0]), "a_ln_b": row(a_ln_b[0]),
        "a_w_s": jnp.where(tril, a_w_s[0], 0.0).astype(BF16),
        "a_b_s": jnp.broadcast_to(a_b_s[0][:, :, None], (A_GROUPS, MLP_CHUNK, LANE)).astype(F32),
        "b_D": row(b_D[0]), "b_glu_w": b_glu_w[0].astype(BF16),
        "odd_w_main": odd_w_in[0][:, :n_main].astype(BF16),
        "odd_w_ab": jnp.pad(odd_w_in[0][:, n_main:], ((0, 0), (0, LANE - 2 * GDN_HEADS))).astype(BF16),
        "odd_w_out": odd_w_out[0].astype(BF16),
        "d_conv_w": d_conv_w[0].astype(F32),
        "d_A_log": pad_lane(d_A_log[0]), "d_dt_bias": pad_lane(d_dt_bias[0]),
        "d_norm_g": row(d_norm_g[0]),
    }
    ar, ai, bbr, bbi = _s5_disc(b_lam_re[0], b_lam_im[0], b_log_dt[0], b_B_re[0], b_B_im[0])
    bre, bim, cre, cim = _s5_block_diag((bbr, bbi), b_C_re[0], b_C_im[0])
    s5w = (ar.reshape(1, S5_LANES), ai.reshape(1, S5_LANES), bre, bim, cre, cim)

    (y_p, b_re_p, b_im_p, _, k_p, v_p, s_p, conv_p) = _trunk(x_prompt, p_prompt, None, wts, s5w)
    past = {"b_re": state_b_re[0], "b_im": state_b_im[0],
            "k_c": jnp.transpose(cache_k_c[0], (0, 2, 3, 1)),
            "v_c": jnp.transpose(cache_v_c[0], (0, 2, 3, 1)),
            "s_d": state_d[0], "conv_d": state_conv_d[0]}
    (y_s, b_re_s, b_im_s, a_v_s, k_s, v_s, s_s, conv_s) = _trunk(x_sample, p_sample, past, wts, s5w)
    return (y_p, y_s, b_re_p, b_im_p, k_p, v_p, s_p, conv_p,
            b_re_s, b_im_s, a_v_s, k_s, v_s, s_s, conv_s)
```

```python
import functools
import math

import jax
import jax.numpy as jnp
from jax import lax
from jax.experimental import pallas as pl
from jax.experimental.pallas import tpu as pltpu

F32 = jnp.float32
BF16 = jnp.bfloat16
EPS = 1e-6

D_MODEL = 1024
PLE_DIM = 256
MIX_W = 512
A_GROUPS = 4
MLP_CHUNK = 128
S5_GROUPS = 32
S5_P = 16
S5_N = 64
S5_LANES = S5_GROUPS * S5_N
SB_HEADS = 8
SB_DIM = 64
SB_BLOCK = 128
GDN_HEADS = 4
GDN_DIM = 128
GDN_CHUNK = 64
CONV_W = 4
CONV_CH = 3 * MIX_W
LANE = 128
SUBLANE = 8
VMEM_LIMIT = 52 * 1024 * 1024
SB_DEAD = -104.0
SB_MASKED = -1e9


def _cparams(sem):
    return pltpu.CompilerParams(dimension_semantics=sem, vmem_limit_bytes=VMEM_LIMIT)


def _gelu(x):
    return 0.5 * x * (1.0 + jnp.tanh(0.7978845608028654 * (x + 0.044715 * (x * x * x))))


def _sigmoid(x):
    return 1.0 / (1.0 + jnp.exp(-x))


def _silu(x):
    return x * _sigmoid(x)


def _softplus(x):
    return jnp.maximum(x, 0.0) + jnp.log1p(jnp.exp(-jnp.abs(x)))


def _rms(x, g):
    ms = jnp.mean(x * x, axis=-1, keepdims=True)
    return x * lax.rsqrt(ms + EPS) * g


def _dot(a, b):
    return jnp.dot(a, b, preferred_element_type=F32)


def _dot_nt(a, b):
    return lax.dot_general(a, b, (((1,), (1,)), ((), ())), preferred_element_type=F32)


def _dot_tn(a, b):
    return lax.dot_general(a, b, (((0,), (0,)), ((), ())), preferred_element_type=F32)


def _split3(x):
    hi = x.astype(BF16)
    r = x - hi.astype(F32)
    mid = r.astype(BF16)
    lo = (r - mid.astype(F32)).astype(BF16)
    return hi, mid, lo


def _full(shape):
    n = len(shape)
    return pl.BlockSpec(shape, lambda *_: (0,) * n)


def _time_major(x, nseq):
    tm = x.shape[0] // nseq
    return pltpu.einshape("btd->tbd", x.reshape(nseq, tm, x.shape[1]))


def _even_in_body(x_ref, g_ref, w_ref, lng_ref, lnb_ref, ws_ref, bs_ref,
                  aout_ref, ub_ref, szb_ref, av_ref, *, chunk):
    nseq, tm, _ = x_ref.shape
    rows_all = nseq * tm
    hn = _rms(x_ref[...].reshape(rows_all, D_MODEL), g_ref[...]).astype(BF16)

    def proj(k):
        return _dot(hn, w_ref[:, k * MIX_W:(k + 1) * MIX_W])

    va = _gelu(proj(1))
    mu = jnp.mean(va, axis=-1, keepdims=True)
    vc = va - mu
    var = jnp.mean(vc * vc, axis=-1, keepdims=True)
    va = vc * lax.rsqrt(var + EPS) * lng_ref[...] + lnb_ref[...]
    if av_ref is not None:
        av_ref[...] = va.reshape(av_ref.shape)
    va_bf = va.astype(BF16)
    gate = _gelu(proj(0)) * _silu(proj(2))
    gw = MIX_W // A_GROUPS
    pieces = []
    for ch in range(rows_all // chunk):
        rows = slice(ch * chunk, (ch + 1) * chunk)
        s = [_dot(ws_ref[g], va_bf[rows, g * gw:(g + 1) * gw]) + bs_ref[g] for g in range(A_GROUPS)]
        pieces.append(gate[rows] * jnp.concatenate(s, axis=1))
    aout_ref[...] = jnp.concatenate(pieces, axis=0).astype(aout_ref.dtype).reshape(aout_ref.shape)
    ub_ref[...] = _time_major(proj(3), nseq)
    szb_ref[...] = _time_major(_silu(proj(4)), nseq)


def _even_in(x, norm_g, w_in, ln_g, ln_b, w_s, b_s, *, chunk, tm, want_av):
    bsz, length, _ = x.shape
    body = functools.partial(_even_in_body, chunk=chunk)
    if not want_av:
        body_fn = lambda *r: body(*r, None)
    else:
        body_fn = body
    seq = lambda w: pl.BlockSpec((SUBLANE, tm, w), lambda b, i: (b, i, 0))
    tmaj = pl.BlockSpec((tm, SUBLANE, MIX_W), lambda b, i: (i, b, 0))
    out_shape = [jax.ShapeDtypeStruct((bsz, length, MIX_W), BF16),
                 jax.ShapeDtypeStruct((length, bsz, MIX_W), F32),
                 jax.ShapeDtypeStruct((length, bsz, MIX_W), F32)]
    out_specs = [seq(MIX_W), tmaj, tmaj]
    if want_av:
        out_shape.append(jax.ShapeDtypeStruct((bsz, length, MIX_W), F32))
        out_specs.append(seq(MIX_W))
    return pl.pallas_call(
        body_fn,
        grid=(bsz // SUBLANE, length // tm),
        in_specs=[seq(D_MODEL),
                  _full((1, D_MODEL)),
                  _full(w_in.shape),
                  _full((1, MIX_W)), _full((1, MIX_W)),
                  _full(w_s.shape), _full(b_s.shape)],
        out_specs=out_specs,
        out_shape=out_shape,
        compiler_params=_cparams(("parallel", "parallel")),
        name="even_in",
    )(x, norm_g, w_in, ln_g, ln_b, w_s, b_s)


def _s5_disc_body(lr_ref, li_ref, ldt_ref, br_ref, bi_ref, ar_ref, ai_ref, bbr_ref, bbi_ref):
    dt = jnp.exp(ldt_ref[...])
    lr, li = lr_ref[...], li_ref[...]
    mag = jnp.exp(lr * dt)
    ar = mag * jnp.cos(li * dt)
    ai = mag * jnp.sin(li * dt)
    den = lr * lr + li * li
    cr = ((ar - 1.0) * lr + ai * li) / den
    ci = (ai * lr - (ar - 1.0) * li) / den
    ar_ref[...] = ar
    ai_ref[...] = ai
    for p in range(S5_P):
        br, bi = br_ref[p], bi_ref[p]
        bbr_ref[p] = cr * br - ci * bi
        bbi_ref[p] = cr * bi + ci * br


def _s5_disc(lam_re, lam_im, log_dt, b_re, b_im):
    gn = (S5_GROUPS, S5_N)
    pgn = (S5_P, S5_GROUPS, S5_N)
    return pl.pallas_call(
        _s5_disc_body,
        out_shape=[jax.ShapeDtypeStruct(gn, F32), jax.ShapeDtypeStruct(gn, F32),
                   jax.ShapeDtypeStruct(pgn, F32), jax.ShapeDtypeStruct(pgn, F32)],
        name="s5_disc",
    )(lam_re, lam_im, log_dt.reshape(S5_GROUPS, 1),
      jnp.transpose(b_re, (2, 0, 1)), jnp.transpose(b_im, (2, 0, 1)))


def _s5_block_diag(bb_pgn, c_re, c_im):
    gpc = LANE // S5_P
    nj = S5_GROUPS // gpc
    eye = jnp.eye(gpc, dtype=F32)

    def b_blk(bb):
        b = jnp.transpose(bb, (1, 0, 2)).reshape(nj, gpc, S5_P, S5_N)
        m = b[:, :, :, None, :] * eye[None, :, None, :, None]
        return m.reshape(nj, gpc * S5_P, gpc * S5_N).astype(BF16)

    def c_blk(c):
        cc = jnp.transpose(c, (0, 2, 1)).reshape(nj, gpc, S5_N, S5_P)
        m = cc[:, :, :, None, :] * eye[None, :, None, :, None]
        return m.reshape(nj, gpc * S5_N, gpc * S5_P).astype(BF16)

    return b_blk(bb_pgn[0]), b_blk(bb_pgn[1]), c_blk(c_re), c_blk(-c_im)


def _s5_body(u_ref, szb_ref, x0r_ref, x0i_ref, ar_ref, ai_ref, bre_ref, bim_ref, cre_ref, cim_ref,
             d_ref, glu_ref, out_ref, st_r, st_i, sre, sim):
    t_steps = u_ref.shape[0]
    rows = t_steps * SUBLANE
    ti = pl.program_id(1)
    cw = bre_ref.shape[2]
    nj = bre_ref.shape[0]

    @pl.when(ti == 0)
    def _():
        st_r[...] = x0r_ref[...]
        st_i[...] = x0i_ref[...]

    u = u_ref[...].reshape(rows, MIX_W)
    u_bf = u.astype(BF16)
    for j in range(nj):
        uj = u_bf[:, j * LANE:(j + 1) * LANE]
        sre[:, j * cw:(j + 1) * cw] = _dot(uj, bre_ref[j])
        sim[:, j * cw:(j + 1) * cw] = _dot(uj, bim_ref[j])

    for j in range(0, nj, 2):
        lanes = slice(j * cw, (j + 2) * cw)
        a_r = jnp.broadcast_to(ar_ref[:, lanes], (SUBLANE, 2 * cw))
        a_i = jnp.broadcast_to(ai_ref[:, lanes], (SUBLANE, 2 * cw))

        xr, xi = st_r[:, lanes], st_i[:, lanes]
        for t in range(t_steps):
            rows_t = slice(t * SUBLANE, (t + 1) * SUBLANE)
            xr, xi = (a_r * xr - a_i * xi + sre[rows_t, lanes],
                      a_r * xi + a_i * xr + sim[rows_t, lanes])
            sre[rows_t, lanes] = xr
            sim[rows_t, lanes] = xi
        st_r[:, lanes] = xr
        st_i[:, lanes] = xi

    ys = []
    for j in range(nj):
        lanes = slice(j * cw, (j + 1) * cw)
        ys.append(_dot(sre[:, lanes].astype(BF16), cre_ref[j])
                  + _dot(sim[:, lanes].astype(BF16), cim_ref[j]))
    y = jnp.concatenate(ys, axis=1) + d_ref[...] * u
    y = _gelu(y)
    y = y * _sigmoid(_dot(y.astype(BF16), glu_ref[...]))
    y = y * szb_ref[...].reshape(rows, MIX_W)
    out_ref[...] = y.reshape(out_ref.shape)


def _s5(u_tm, szb_tm, x0r, x0i, ar, ai, bre, bim, cre, cim, d_skip, glu_w, *, t_steps):
    length, nb, _ = u_tm.shape
    blk = pl.BlockSpec((t_steps, SUBLANE, MIX_W), lambda b, t: (t, b, 0))
    st = pl.BlockSpec((SUBLANE, S5_LANES), lambda b, t: (b, 0))
    rows = t_steps * SUBLANE
    return pl.pallas_call(
        _s5_body,
        grid=(nb // SUBLANE, length // t_steps),
        in_specs=[blk, blk, st, st, _full((1, S5_LANES)), _full((1, S5_LANES)),
                  _full(bre.shape), _full(bim.shape), _full(cre.shape), _full(cim.shape),
                  _full((1, MIX_W)), _full(glu_w.shape)],
        out_specs=[blk, st, st],
        out_shape=[jax.ShapeDtypeStruct((length, nb, MIX_W), F32),
                   jax.ShapeDtypeStruct((nb, S5_LANES), F32),
                   jax.ShapeDtypeStruct((nb, S5_LANES), F32)],
        scratch_shapes=[pltpu.VMEM((rows, S5_LANES), F32), pltpu.VMEM((rows, S5_LANES), F32)],
        compiler_params=_cparams(("parallel", "arbitrary")),
        name="s5_scan",
    )(u_tm, szb_tm, x0r, x0i, ar, ai, bre, bim, cre, cim, d_skip, glu_w)


def _mix_out_body(a_ref, b_ref, h_ref, p_ref, wout_ref, pg_ref, wgate_ref, wp_ref, fg_ref, out_ref,
                  *, b_time_major):
    rows = out_ref.shape[0] * out_ref.shape[1]
    flat = lambda ref: ref[...].reshape(rows, ref.shape[-1])
    if b_time_major:
        b = jnp.concatenate([b_ref[:, s, :] for s in range(b_ref.shape[1])], axis=0)
    else:
        b = flat(b_ref)
    mix = _dot(flat(a_ref), wout_ref[:MIX_W, :]) + _dot(b.astype(BF16), wout_ref[MIX_W:, :])
    h1 = flat(h_ref) + mix
    gate = _sigmoid(_dot(_rms(h1, pg_ref[...]).astype(BF16), wgate_ref[...]))
    h2 = h1 + gate * _dot(flat(p_ref).astype(BF16), wp_ref[...])
    if fg_ref is not None:
        h2 = _rms(h2, fg_ref[...])
    out_ref[...] = h2.reshape(out_ref.shape)


def _mix_out(a, b, h, p, layer, w_out, ple_g, w_gate, w_p, final_g, *, tm, b_time_major):
    bsz, length, _ = h.shape
    seq = lambda w: pl.BlockSpec((SUBLANE, tm, w), lambda s, i: (s, i, 0))
    b_spec = pl.BlockSpec((tm, SUBLANE, MIX_W), lambda s, i: (i, s, 0)) if b_time_major else seq(MIX_W)
    args = [a, b, h, p, w_out, ple_g, w_gate, w_p]
    in_specs = [seq(MIX_W), b_spec, seq(D_MODEL),
                pl.BlockSpec((None, SUBLANE, tm, PLE_DIM), lambda s, i: (layer, s, i, 0)),
                _full(w_out.shape), _full((1, D_MODEL)), _full(w_gate.shape), _full(w_p.shape)]
    body = functools.partial(_mix_out_body, b_time_major=b_time_major)
    if final_g is None:
        body_fn = lambda *r: body(*r[:8], None, r[8])
    else:
        body_fn = body
        args.append(final_g)
        in_specs.append(_full((1, D_MODEL)))
    return pl.pallas_call(
        body_fn,
        grid=(bsz // SUBLANE, length // tm),
        in_specs=in_specs,
        out_specs=seq(D_MODEL),
        out_shape=jax.ShapeDtypeStruct((bsz, length, D_MODEL), F32),
        compiler_params=_cparams(("parallel", "parallel")),
        name="mix_out",
    )(*args)


def _odd_in_body(h_ref, g_ref, w_ref, wab_ref, alog_ref, dtb_ref,
                 q_ref, k_ref, v_ref, szc_ref, qkv_ref, szd_ref, gb_ref, k4_ref, v4_ref):
    hn = _rms(h_ref[...], g_ref[...]).astype(BF16)

    def proj(k, width=MIX_W):
        return _dot(hn, w_ref[:, k * MIX_W:k * MIX_W + width])

    def head_major(x):
        return pltpu.einshape("t(hd)->thd", x, h=SB_HEADS)

    q_ref[...] = (proj(0) * (SB_DIM ** -0.5)).astype(BF16)
    kc, vc = proj(1), proj(2)
    k_ref[...] = kc.astype(BF16)
    v_ref[...] = vc.astype(BF16)
    k4_ref[...] = head_major(kc)
    v4_ref[...] = head_major(vc)
    szc_ref[...] = _silu(proj(3))
    for k in range(3):
        qkv_ref[:, k * MIX_W:(k + 1) * MIX_W] = proj(4 + k)
    szd_ref[...] = _silu(proj(7))
    ab = _dot(hn, wab_ref[...])
    g = -jnp.exp(alog_ref[...]) * _softplus(ab + dtb_ref[...])
    lane = lax.broadcasted_iota(jnp.int32, ab.shape, 1)
    gb_ref[...] = jnp.where(lane < GDN_HEADS, g, _sigmoid(ab))


def _odd_in(h2d, norm_g, w_main, w_ab, a_log, dt_bias, *, tm):
    n = h2d.shape[0]
    row = lambda i: (i, 0)
    widths = [MIX_W, MIX_W, MIX_W, MIX_W, CONV_CH, MIX_W, LANE]
    dtypes = [BF16, BF16, BF16, F32, F32, F32, F32]
    head4 = pl.BlockSpec((tm, SB_HEADS, SB_DIM), lambda i: (i, 0, 0))
    head4_shape = jax.ShapeDtypeStruct((n, SB_HEADS, SB_DIM), F32)
    return pl.pallas_call(
        _odd_in_body,
        grid=(n // tm,),
        in_specs=[pl.BlockSpec((tm, D_MODEL), row), _full((1, D_MODEL)),
                  _full(w_main.shape), _full(w_ab.shape), _full((1, LANE)), _full((1, LANE))],
        out_specs=[pl.BlockSpec((tm, w), row) for w in widths] + [head4, head4],
        out_shape=[jax.ShapeDtypeStruct((n, w), d) for w, d in zip(widths, dtypes)]
        + [head4_shape, head4_shape],
        compiler_params=_cparams(("parallel",)),
        name="odd_in",
    )(h2d, norm_g, w_main, w_ab, a_log, dt_bias)


def _sb_pairs(q2x, k_blocks, v_blocks, masks, valids, c_in, cum_rhs, key_major):
    npair = len(q2x)
    nblk = len(k_blocks[0])
    rows = q2x[0].shape[0]
    if key_major:
        z = [_dot(q2x[p], jnp.concatenate(k_blocks[p], axis=1)) for p in range(npair)]
    else:
        z = [_dot_nt(q2x[p], jnp.concatenate(k_blocks[p], axis=0)) for p in range(npair)]
    log_beta, hilo = [], []
    for p in range(npair):
        lb, hl = [], []
        for j in range(nblk):
            zj = z[p][:, j * LANE:(j + 1) * LANE]
            if masks[j] is not None:
                zj = jnp.where(masks[j], zj, SB_MASKED)
            if valids[p][j] is not None:
                zj = jnp.where(valids[p][j], zj, SB_MASKED)
            l1p = jnp.log(1.0 + jnp.exp(-jnp.abs(zj)))
            lb.append(jnp.minimum(zj, 0.0) - l1p)
            log_keep = lb[-1] - zj
            hi = log_keep.astype(BF16)
            lo = (log_keep - hi.astype(F32)).astype(BF16)
            hl.append(jnp.concatenate([hi, lo], axis=1))
        log_beta.append(lb)
        hilo.append(jnp.concatenate(hl, axis=0))
    cum = [_dot(x, cum_rhs) for x in hilo]
    ws, totals = [], []
    for p in range(npair):
        c = c_in[p]
        w_p = [None] * nblk
        for j in reversed(range(nblk)):
            after = cum[p][j * rows:(j + 1) * rows, :LANE]
            if c is not None:
                after = after + c
            w_p[j] = jnp.exp(log_beta[p][j] + after).astype(BF16)
            total = cum[p][j * rows:(j + 1) * rows, LANE:]
            c = total if c is None else c + total
        ws.append(jnp.concatenate(w_p, axis=1))
        totals.append(c)
    if key_major:
        pv = [_dot_nt(ws[p], jnp.concatenate(v_blocks[p], axis=1)) for p in range(npair)]
    else:
        pv = [_dot(ws[p], jnp.concatenate(v_blocks[p], axis=0)) for p in range(npair)]
    return pv, totals


def _sb_body(q_ref, kd_ref, vd_ref, k1_ref, v1_ref, k2_ref, v2_ref, szc_ref, cum_ref,
             kold_ref, vold_ref, out_ref, acc_ref, c_ref, kbuf, vbuf, sem, *, top_of, key_major):
    b = pl.program_id(0)
    i = pl.program_id(1)
    tq = q_ref.shape[0]
    tk = kbuf.shape[-1] if key_major else kbuf.shape[0]
    npair = MIX_W // LANE
    pairs = range(npair)
    tt = lax.broadcasted_iota(jnp.int32, (2 * tq, tk), 0)
    ss = lax.broadcasted_iota(jnp.int32, (2 * tq, tk), 1)
    diag_mask = ss < jnp.where(tt >= tq, tt - tq, tt)
    even_lane = lax.broadcasted_iota(jnp.int32, (tq, LANE), 1) < SB_DIM
    top = top_of(i)

    def diag_tile(ref, p):
        x = ref[:, p * LANE:(p + 1) * LANE]
        if key_major:
            x = x.astype(F32).T
            if tq < tk:
                x = jnp.concatenate([x, jnp.zeros((LANE, tk - tq), F32)], axis=1)
        elif tq < tk:
            x = jnp.concatenate([x, jnp.zeros((tk - tq, LANE), x.dtype)], axis=0)
        return x.astype(BF16)

    def old_tile(ref, p):
        if key_major:
            return jnp.concatenate([ref[2 * p], ref[2 * p + 1]], axis=0).astype(BF16)
        return ref[:, p * LANE:(p + 1) * LANE].astype(BF16)

    def q_pair(p):
        q2 = q_ref[:, p * LANE:(p + 1) * LANE]
        zero = jnp.zeros_like(q2)
        return jnp.concatenate([jnp.where(even_lane, q2, zero), jnp.where(even_lane, zero, q2)], axis=0)

    def merge(pv):
        return jnp.where(even_lane, pv[:tq], pv[tq:])

    q2x = [q_pair(p) for p in pairs]
    k_win = [[old_tile(k2_ref, p), old_tile(k1_ref, p), diag_tile(kd_ref, p)] for p in pairs]
    v_win = [[old_tile(v2_ref, p), old_tile(v1_ref, p), diag_tile(vd_ref, p)] for p in pairs]
    masks = [None, None, diag_mask]
    valids = [[top >= 1, top >= 0, None]] * npair
    cum_rhs = cum_ref[...]
    pvs, sums = _sb_pairs(q2x, k_win, v_win, masks, valids, [None] * npair, cum_rhs, key_major)
    acc_all = jnp.concatenate([merge(pv) for pv in pvs], axis=1)
    out_ref[...] = (acc_all * szc_ref[...]).astype(out_ref.dtype)
    c_max = functools.reduce(jnp.maximum, sums)

    @pl.when(jnp.logical_and(top >= 2, jnp.max(c_max) > SB_DEAD))
    def _():
        acc_ref[...] = acc_all
        for p in pairs:
            c_ref[p] = sums[p]

        def cond(carry):
            j, go = carry
            return jnp.logical_and(j >= 0, go)

        def body(carry):
            j, _ = carry
            r0 = pl.multiple_of(j * tk, tk)
            if key_major:
                src = lambda ref: ref.at[b, :, :, pl.ds(r0, tk)]
            else:
                src = lambda ref: ref.at[b, pl.ds(r0, tk)]
            ck = pltpu.make_async_copy(src(kold_ref), kbuf, sem.at[0])
            cv = pltpu.make_async_copy(src(vold_ref), vbuf, sem.at[1])
            ck.start()
            cv.start()
            ck.wait()
            cv.wait()
            pvs, cs = _sb_pairs(q2x, [[old_tile(kbuf, p)] for p in pairs],
                                [[old_tile(vbuf, p)] for p in pairs], [None], [[None]] * npair,
                                [c_ref[p] for p in pairs], cum_rhs, key_major)
            for p in pairs:
                acc_ref[:, p * LANE:(p + 1) * LANE] += merge(pvs[p])
                c_ref[p] = cs[p]
            return j - 1, jnp.max(c_ref[...]) > SB_DEAD

        lax.while_loop(cond, body, (top - 2, jnp.max(c_max) > SB_DEAD))
        out_ref[...] = (acc_ref[...] * szc_ref[...]).astype(out_ref.dtype)


def _sb_cum_rhs(tk):
    jj = lax.broadcasted_iota(jnp.int32, (tk, tk), 0)
    ss = lax.broadcasted_iota(jnp.int32, (tk, tk), 1)
    half = jnp.concatenate([(jj > ss).astype(BF16), jnp.ones((tk, LANE), BF16)], axis=1)
    return jnp.concatenate([half, half], axis=0)


def _sb_attention(q, k_new, v_new, szc, k_past=None, v_past=None):
    bsz, length, _ = q.shape
    tq = min(length, SB_BLOCK)
    nq = length // tq
    tk = SB_BLOCK
    key_major = k_past is not None
    if not key_major:
        assert tq == tk
        k_old, v_old = k_new, v_new
        top_of = lambda i: i - 1
        old_blk = (tk, MIX_W)
        old_idx = lambda b, blk: (b, blk, 0)
    else:
        assert nq == 1 and k_past.shape[-1] % tk == 0
        k_old, v_old = k_past, v_past
        n_old = k_past.shape[-1] // tk
        top_of = lambda i: i * 0 + (n_old - 1)
        old_blk = (SB_HEADS, SB_DIM, tk)
        old_idx = lambda b, blk: (b, 0, 0, blk)
    cur = pl.BlockSpec((None, tq, MIX_W), lambda b, i: (b, i, 0))
    old1 = pl.BlockSpec((None,) + old_blk, lambda b, i: old_idx(b, jnp.maximum(top_of(i), 0)))
    old2 = pl.BlockSpec((None,) + old_blk, lambda b, i: old_idx(b, jnp.maximum(top_of(i) - 1, 0)))
    anyspec = pl.BlockSpec(memory_space=pl.ANY)
    cum_rhs = _sb_cum_rhs(tk)
    return pl.pallas_call(
        functools.partial(_sb_body, top_of=top_of, key_major=key_major),
        grid=(bsz, nq),
        in_specs=[cur, cur, cur, old1, old1, old2, old2, cur, _full(cum_rhs.shape),
                  anyspec, anyspec],
        out_specs=cur,
        out_shape=jax.ShapeDtypeStruct((bsz, length, MIX_W), BF16),
        scratch_shapes=[pltpu.VMEM((tq, MIX_W), F32),
                        pltpu.VMEM((MIX_W // LANE, 2 * tq, LANE), F32),
                        pltpu.VMEM(old_blk, k_old.dtype), pltpu.VMEM(old_blk, v_old.dtype),
                        pltpu.SemaphoreType.DMA((2,))],
        compiler_params=_cparams(("parallel", "parallel")),
        name="sb_attention",
    )(q, k_new, v_new, k_old, v_old, k_old, v_old, szc, cum_rhs, k_old, v_old)


def _dot01(m01, x):
    hi, mid, lo = _split3(x)
    return _dot(m01, hi) + _dot(m01, mid) + _dot(m01, lo)


def _dot01_r(x, m01):
    hi, mid, lo = _split3(x)
    return _dot(hi, m01) + _dot(mid, m01) + _dot(lo, m01)


def _gdn_body(cur_ref, prev_ref, buf_ref, gb_ref, szd_ref, s0_ref, cw_ref, e64_ref, e128_ref,
              bd_ref, ng_ref, out_ref, s_ref):
    c = GDN_CHUNK
    nseq = cur_ref.shape[0]
    rows = nseq * c
    nh, hw = GDN_HEADS, GDN_DIM
    hc = nh * c
    sh = int(math.log2(c))
    first = pl.program_id(1) == 0
    chunk_rows = [slice(s * c, (s + 1) * c) for s in range(nseq)]

    @pl.when(first)
    def _():
        s_ref[...] = s0_ref[...]

    def conv_silu(part):
        cols = slice(part * MIX_W, (part + 1) * MIX_W)
        cur = cur_ref[:, :, cols].reshape(rows, MIX_W)
        cw = cw_ref[:, cols]

        def conv(shifted):
            acc = shifted(CONV_W - 1) * cw[0:1, :]
            for t in range(1, CONV_W):
                acc = acc + shifted(CONV_W - 1 - t) * cw[t:t + 1, :]
            return acc

        y = conv(lambda d: pltpu.roll(cur, d, axis=0) if d else cur)
        pieces = []
        for s, rs in enumerate(chunk_rows):
            prev = jnp.where(first, buf_ref[s, :, cols], prev_ref[s, :, cols])
            head = jnp.concatenate([prev, cur[rs.start:rs.start + SUBLANE]], axis=0)
            pieces.append(conv(lambda d, head=head: head[SUBLANE - d:2 * SUBLANE - d]))
            pieces.append(y[rs.start + SUBLANE:rs.stop])
        return _silu(jnp.concatenate(pieces, axis=0))

    def l2n(x, scale):
        parts = []
        for h in range(nh):
            xh = x[:, h * hw:(h + 1) * hw]
            parts.append(xh * (lax.rsqrt(jnp.sum(xh * xh, axis=-1, keepdims=True) + EPS) * scale))
        return jnp.concatenate(parts, axis=1)

    k = l2n(conv_silu(1), 1.0)
    k_bf = k.astype(BF16)

    gb = gb_ref[...].reshape(rows, LANE)
    r_i = lax.broadcasted_iota(jnp.int32, (rows, rows), 0)
    c_i = lax.broadcasted_iota(jnp.int32, (rows, rows), 1)
    same_chunk = lax.shift_right_logical(r_i, sh) == lax.shift_right_logical(c_i, sh)
    g_all = _dot01(jnp.logical_and(same_chunk, c_i <= r_i).astype(BF16), gb)
    lane = lax.broadcasted_iota(jnp.int32, gb.shape, 1)
    x = jnp.where(lane < nh, g_all, gb)
    s64 = _dot01_r(x, e64_ref[...])
    s128 = _dot01_r(x, e128_ref[...])
    eg_all = jnp.exp(s128[:, :MIX_W])

    t_i = lax.broadcasted_iota(jnp.int32, (c, hc), 0)
    s_i = jnp.bitwise_and(lax.broadcasted_iota(jnp.int32, (c, hc), 1), c - 1)
    causal = s_i <= t_i
    strict = s_i < t_i
    eye = s_i == t_i
    blk_of_lane = lax.shift_right_logical(lax.broadcasted_iota(jnp.int32, (c, hc), 1), sh)
    bd_mask = bd_ref[...]

    def block_diag(x_bf):
        return jnp.concatenate([x_bf] * nh, axis=0) * bd_mask

    def diag_blocks(p):
        out = None
        for h in range(nh):
            sel = jnp.where(blk_of_lane == h, p[h * c:(h + 1) * c, :], 0.0)
            out = sel if out is None else out + sel
        return out

    def heads_on_rows(x_bf):
        return jnp.concatenate([x_bf[:, h * hw:(h + 1) * hw] for h in range(nh)], axis=0)

    tinv, pw, decay, k_rows = [], [], [], []
    for rs in chunk_rows:
        g64, b64 = s64[rs, :hc], s64[rs, hc:]
        g_row = jnp.sum(jnp.where(eye, g64, 0.0), axis=0, keepdims=True)
        decay.append(jnp.exp(jnp.where(causal, g64 - g_row, -1e30)))
        k_rows.append(heads_on_rows(k_bf[rs]))
        nmat = jnp.where(strict, b64 * decay[-1] * diag_blocks(_dot_nt(k_rows[-1], k_rows[-1])), 0.0)
        tinv.append(jnp.where(eye, 1.0, 0.0) - nmat)
        pw.append(nmat.astype(BF16))

    state = {}

    def fill_q():
        state["q"] = l2n(conv_silu(0), hw ** -0.5).astype(BF16)

    def fill_qk():
        state["aqk"] = []
        for rs, kr, dc in zip(chunk_rows, k_rows, decay):
            qk = diag_blocks(_dot_nt(heads_on_rows(state["q"][rs]), kr))
            state["aqk"].append(jnp.where(causal, qk * dc, 0.0).astype(BF16))

    def fill_v():
        state["v"] = conv_silu(2)

    def fill_rhs():
        state["rhs"] = []
        for rs in chunk_rows:
            b128 = s128[rs, MIX_W:]
            bk = (b128 * eg_all[rs] * k[rs]).astype(BF16)
            bv = (b128 * state["v"][rs]).astype(BF16)
            state["rhs"].append(jnp.concatenate([heads_on_rows(bk), heads_on_rows(bv)], axis=1))

    def fill_kd():
        state["kd"] = []
        for rs in chunk_rows:
            g128 = s128[rs, :MIX_W]
            state["kd"].append((k[rs] * jnp.exp(g128[c - 1:c, :] - g128)).astype(BF16))

    fillers = [fill_q, fill_qk, fill_v, fill_rhs, fill_kd]
    assert len(fillers) == sh - 1
    for level in range(sh - 1):
        pw = [_dot(p, block_diag(p)).astype(BF16) for p in pw]
        fillers[level]()
        tinv = [t + _dot(t.astype(BF16), block_diag(p)) for t, p in zip(tinv, pw)]
    wu = [_dot(block_diag(t.astype(BF16)), rhs) for t, rhs in zip(tinv, state["rhs"])]

    items = [(s, h) for s in range(nseq) for h in range(nh)]
    cols = lambda h: slice(h * hw, (h + 1) * hw)
    hrow = lambda h: slice(h * c, (h + 1) * c)
    s_bf = [s_ref[s, h].astype(BF16) for s, h in items]
    ws_qs = [_dot(jnp.concatenate([wu[s][hrow(h), :hw].astype(BF16),
                                   state["q"][chunk_rows[s], cols(h)]], axis=0), sb)
             for (s, h), sb in zip(items, s_bf)]
    d_bf = [(wu[s][hrow(h), hw:] - x[:c]).astype(BF16) for (s, h), x in zip(items, ws_qs)]
    upd = [_dot(jnp.concatenate([state["aqk"][s][:, hrow(h)], state["kd"][s][:, cols(h)].T], axis=0),
                d) for (s, h), d in zip(items, d_bf)]
    for (s, h), y in zip(items, upd):
        rs = chunk_rows[s]
        s_ref[s, h] = eg_all[rs.stop - 1:rs.stop, cols(h)] * s_ref[s, h] + y[c:]
    o = [eg_all[chunk_rows[s], cols(h)] * x[c:] + y[:c] for (s, h), x, y in zip(items, ws_qs, upd)]
    ms = [jnp.mean(x * x, axis=-1, keepdims=True) for x in o]
    for (s, h), x, m in zip(items, o, ms):
        out_ref[s, :, cols(h)] = (x * lax.rsqrt(m + EPS) * ng_ref[...]
                                  * szd_ref[s, :, cols(h)]).astype(out_ref.dtype)


def _gdn(qkv, conv_buf8, conv_w, gb, szd, s0, norm_g):
    bsz, length, _ = qkv.shape
    c, nh, hw = GDN_CHUNK, GDN_HEADS, GDN_DIM
    nseq = GDN_SEQS
    per = c // SUBLANE
    src = jnp.arange(LANE)[:, None]

    def spread(width):
        dst = jnp.arange(2 * nh * width)[None, :]
        return (src == dst // width).astype(BF16)

    e64, e128 = spread(c), spread(hw)
    blk = jnp.arange(nh * c) // c
    bd_mask = (blk[:, None] == blk[None, :]).astype(BF16)
    cur = lambda w: pl.BlockSpec((nseq, c, w), lambda b, i: (b, i, 0))
    prev = pl.BlockSpec((nseq, SUBLANE, CONV_CH), lambda b, i: (b, jnp.maximum(i * per - 1, 0), 0))
    buf = pl.BlockSpec((nseq, SUBLANE, CONV_CH), lambda b, i: (b, 0, 0))
    st = pl.BlockSpec((nseq, nh, hw, hw), lambda b, i: (b, 0, 0, 0))
    return pl.pallas_call(
        _gdn_body,
        grid=(bsz // nseq, length // c),
        in_specs=[cur(CONV_CH), prev, buf, cur(LANE), cur(MIX_W), st, _full(conv_w.shape),
                  _full(e64.shape), _full(e128.shape), _full(bd_mask.shape), _full((1, hw))],
        out_specs=[cur(MIX_W), st],
        out_shape=[jax.ShapeDtypeStruct((bsz, length, MIX_W), BF16),
                   jax.ShapeDtypeStruct(s0.shape, F32)],
        compiler_params=_cparams(("parallel", "arbitrary")),
        name="gdn",
    )(qkv, qkv, conv_buf8, gb, szd, s0, conv_w, e64, e128, bd_mask, norm_g)


TIME_TILE = 128
ROW_TILE = 512
GDN_SEQS = 8


def _trunk(x, p, past, wts, s5w):
    bsz, length, _ = x.shape
    n = bsz * length
    first = past is None
    chunk = min(length, MLP_CHUNK)
    tm = min(length, TIME_TILE)

    outs = _even_in(x, wts["norm_g0"], wts["even_w_in"], wts["a_ln_g"], wts["a_ln_b"],
                    wts["a_w_s"][:, :chunk, :chunk], wts["a_b_s"][:, :chunk, :],
                    chunk=chunk, tm=tm, want_av=not first)
    a_out, ub_tm, szb_tm = outs[:3]
    a_v = None if first else outs[3][None]
    if first:
        x0r = jnp.zeros((bsz, S5_LANES), F32)
        x0i = x0r
    else:
        x0r = past["b_re"].reshape(bsz, S5_LANES)
        x0i = past["b_im"].reshape(bsz, S5_LANES)
    b_out_tm, xr, xi = _s5(ub_tm, szb_tm, x0r, x0i, *s5w, wts["b_D"], wts["b_glu_w"], t_steps=tm)
    h = _mix_out(a_out, b_out_tm, x, p, 0, wts["even_w_out"], wts["ple_norm_g0"], wts["ple_gate_w0"],
                 wts["ple_proj0"], None, tm=tm, b_time_major=True)
    h = h.reshape(n, D_MODEL)

    q, k, v, szc, qkv, szd, gb, k4, v4 = _odd_in(h, wts["norm_g1"], wts["odd_w_main"],
                                                 wts["odd_w_ab"], wts["d_A_log"], wts["d_dt_bias"],
                                                 tm=min(n, ROW_TILE))
    to3 = lambda a: a.reshape(bsz, length, a.shape[-1])
    q, k, v, szc, qkv, szd, gb = map(to3, (q, k, v, szc, qkv, szd, gb))
    if first:
        c_out = _sb_attention(q, k, v, szc)
        buf8 = jnp.zeros((bsz, SUBLANE, CONV_CH), F32)
        s0 = jnp.zeros((bsz, GDN_HEADS, GDN_DIM, GDN_DIM), F32)
    else:
        c_out = _sb_attention(q, k, v, szc, past["k_c"], past["v_c"])
        buf8 = jnp.pad(past["conv_d"], ((0, 0), (SUBLANE - (CONV_W - 1), 0), (0, 0)))
        s0 = past["s_d"]
    d_out, s_new = _gdn(qkv, buf8, wts["d_conv_w"], gb, szd, s0, wts["d_norm_g"])
    y = _mix_out(c_out, d_out, h.reshape(bsz, length, D_MODEL), p, 1, wts["odd_w_out"],
                 wts["ple_norm_g1"], wts["ple_gate_w1"], wts["ple_proj1"], wts["final_norm_g"],
                 tm=tm, b_time_major=False)

    conv_new = qkv[:, length - (CONV_W - 1):, :]
    return (y,
            xr.reshape(1, bsz, S5_GROUPS, S5_N), xi.reshape(1, bsz, S5_GROUPS, S5_N), a_v,
            k4.reshape(1, bsz, length, SB_HEADS, SB_DIM), v4.reshape(1, bsz, length, SB_HEADS, SB_DIM),
            s_new[None], conv_new[None])


def kernel(x_prompt, x_sample, state_b_re, state_b_im, cache_k_c, cache_v_c, state_d, state_conv_d,
           p_prompt, p_sample,
           norm_g, final_norm_g, ple_proj, ple_gate_w, ple_norm_g,
           even_w_in, even_w_out, a_ln_g, a_ln_b, a_w_s, a_b_s,
           b_lam_re, b_lam_im, b_log_dt, b_B_re, b_B_im, b_C_re, b_C_im, b_D, b_glu_w,
           odd_w_in, odd_w_out, d_conv_w, d_A_log, d_dt_bias, d_norm_g):
    row = lambda a: a.reshape(1, -1).astype(F32)
    tril = jnp.tril(jnp.ones((MLP_CHUNK, MLP_CHUNK), dtype=bool))
    n_main = 8 * MIX_W
    pad_lane = lambda a: jnp.pad(a.reshape(1, -1).astype(F32), ((0, 0), (0, LANE - a.size)))
    wts = {
        "norm_g0": row(norm_g[0]), "norm_g1": row(norm_g[1]), "final_norm_g": row(final_norm_g),
        "ple_norm_g0": row(ple_norm_g[0]), "ple_norm_g1": row(ple_norm_g[1]),
        "ple_gate_w0": ple_gate_w[0].astype(BF16), "ple_gate_w1": ple_gate_w[1].astype(BF16),
        "ple_proj0": ple_proj[0].astype(BF16), "ple_proj1": ple_proj[1].astype(BF16),
        "even_w_in": even_w_in[0].astype(BF16), "even_w_out": even_w_out[0].astype(BF16),
        "a_ln_g": row(a_ln_g[0]), "a_ln_b": row(a_ln_b[0]),
        "a_w_s": jnp.where(tril, a_w_s[0], 0.0).astype(BF16),
        "a_b_s": jnp.broadcast_to(a_b_s[0][:, :, None], (A_GROUPS, MLP_CHUNK, LANE)).astype(F32),
        "b_D": row(b_D[0]), "b_glu_w": b_glu_w[0].astype(BF16),
        "odd_w_main": odd_w_in[0][:, :n_main].astype(BF16),
        "odd_w_ab": jnp.pad(odd_w_in[0][:, n_main:], ((0, 0), (0, LANE - 2 * GDN_HEADS))).astype(BF16),
        "odd_w_out": odd_w_out[0].astype(BF16),
        "d_conv_w": d_conv_w[0].astype(F32),
        "d_A_log": pad_lane(d_A_log[0]), "d_dt_bias": pad_lane(d_dt_bias[0]),
        "d_norm_g": row(d_norm_g[0]),
    }
    ar, ai, bbr, bbi = _s5_disc(b_lam_re[0], b_lam_im[0], b_log_dt[0], b_B_re[0], b_B_im[0])
    bre, bim, cre, cim = _s5_block_diag((bbr, bbi), b_C_re[0], b_C_im[0])
    s5w = (ar.reshape(1, S5_LANES), ai.reshape(1, S5_LANES), bre, bim, cre, cim)

    (y_p, b_re_p, b_im_p, _, k_p, v_p, s_p, conv_p) = _trunk(x_prompt, p_prompt, None, wts, s5w)
    past = {"b_re": state_b_re[0], "b_im": state_b_im[0],
            "k_c": jnp.transpose(cache_k_c[0], (0, 2, 3, 1)),
            "v_c": jnp.transpose(cache_v_c[0], (0, 2, 3, 1)),
            "s_d": state_d[0], "conv_d": state_conv_d[0]}
    (y_s, b_re_s, b_im_s, a_v_s, k_s, v_s, s_s, conv_s) = _trunk(x_sample, p_sample, past, wts, s5w)
    return (y_p, y_s, b_re_p, b_im_p, k_p, v_p, s_p, conv_p,
            b_re_s, b_im_s, a_v_s, k_s, v_s, s_s, conv_s)
```
